```python
import math, functools
import jax, jax.numpy as jnp
from jax import lax
import numpy as np

D_MODEL = 1024
BATCH = 8
SEQ = 8192
DEPTH = 1
DEC_BATCH = 128
DEC_SEQ = 4
PAST_LEN = 8192
PAGE_SIZE = 128

DA_HEAD_DIM = 64
DA_HEADS = D_MODEL // (2 * DA_HEAD_DIM)
DA_V_DIM = 2 * DA_HEAD_DIM
DA_QK_W = DA_HEADS * 2 * DA_HEAD_DIM
DA_V_W = DA_HEADS * DA_V_DIM
HG_K = 128
HG_HEADS = D_MODEL // HG_K
HG_V = D_MODEL // HG_HEADS
HG_K_W = HG_HEADS * HG_K
HG_V_W = HG_HEADS * HG_V
HG_CHUNK = 16
D_FF = ((8 * D_MODEL + 3 * 256 - 1) // (3 * 256)) * 256
PLE_DIM = 256
ROPE_THETA = 10000.0
ATT_BLOCK = 128
EPS = 1e-6
NEG = -1e30
N_IN = 2 * DA_QK_W + DA_V_W + 2 * HG_K_W + 2 * HG_V_W + 2 * D_MODEL

kernel_name = 'diffattn_hgrn2_parallel_decoder_step'


def _rmsnorm(x, g):
    xf = x.astype(jnp.float32)
    y = xf * lax.rsqrt(jnp.mean(xf * xf, axis=-1, keepdims=True) + EPS)
    return (y * g.astype(jnp.float32)).astype(x.dtype)


def _rope(x, pos):
    half = x.shape[-1] // 2
    inv = 1.0 / (ROPE_THETA ** (jnp.arange(half, dtype=jnp.float32) / half))
    ang = pos.astype(jnp.float32)[:, None] * inv[None, :]
    cos = jnp.cos(ang)[:, None, None, :]
    sin = jnp.sin(ang)[:, None, None, :]
    xf = x.astype(jnp.float32)
    x1, x2 = xf[..., :half], xf[..., half:]
    return jnp.concatenate([x1 * cos - x2 * sin, x2 * cos + x1 * sin], axis=-1).astype(x.dtype)


def _init_carry(B, H, T):
    return (jnp.full((B, H, 2, T), NEG, jnp.float32),
            jnp.zeros((B, H, 2, T), jnp.float32),
            jnp.zeros((B, H, 2, T, DA_V_DIM), jnp.float32))


def _online_update(carry, q, k, v, mask):
    m, l, acc = carry
    s = jnp.einsum('bqhjd,bkhjd->bhjqk', q, k) * (DA_HEAD_DIM ** -0.5)
    s = jnp.where(mask, s, NEG)
    m_new = jnp.maximum(m, jnp.max(s, axis=-1))
    p = jnp.where(mask, jnp.exp(s - m_new[..., None]), 0.0)
    alpha = jnp.exp(m - m_new)
    l = l * alpha + jnp.sum(p, axis=-1)
    acc = acc * alpha[..., None] + jnp.einsum('bhjqk,bkhe->bhjqe', p, v)
    return (m_new, l, acc)


def _diff_attn_prompt(q, k, v):
    B, S, H = q.shape[0], q.shape[1], q.shape[2]
    nb = S // ATT_BLOCK

    def blocks(a):
        return jnp.moveaxis(a.astype(jnp.float32).reshape((B, nb, ATT_BLOCK) + a.shape[2:]), 1, 0)

    qb, kb, vb = blocks(q), blocks(k), blocks(v)
    tri = jnp.tril(jnp.ones((ATT_BLOCK, ATT_BLOCK), bool))

    def one_query_block(args):
        i, qi = args

        def kstep(carry, inp):
            j, kj, vj = inp
            mask = (j < i) | ((j == i) & tri)
            return _online_update(carry, qi, kj, vj, mask), None

        (m, l, acc), _ = lax.scan(kstep, _init_carry(B, H, ATT_BLOCK), (jnp.arange(nb), kb, vb))
        return acc / l[..., None]

    out = lax.map(one_query_block, (jnp.arange(nb), qb))
    return jnp.moveaxis(out, 0, 3).reshape(B, H, 2, S, DA_V_DIM)


def _diff_attn_sample(q, k, v, cache_k, cache_v, page_table, layer):
    B, T, H = q.shape[0], q.shape[1], q.shape[2]
    qf = q.astype(jnp.float32)

    def pstep(carry, pages):
        kj = cache_k[layer, pages].astype(jnp.float32)
        vj = cache_v[layer, pages].astype(jnp.float32)
        return _online_update(carry, qf, kj, vj, jnp.ones((), bool)), None

    carry, _ = lax.scan(pstep, _init_carry(B, H, T), page_table.T)
    tri = jnp.tril(jnp.ones((T, T), bool))
    m, l, acc = _online_update(carry, qf, k.astype(jnp.float32), v.astype(jnp.float32), tri)
    return acc / l[..., None]


def _gla_chunked(q, k, v, logf, S0):
    B, T, H, K = q.shape
    V = v.shape[-1]
    C = math.gcd(T, HG_CHUNK)
    N = T // C

    def chunks(a):
        return jnp.moveaxis(a.reshape(B, N, C, H, a.shape[-1]), 1, 0)

    causal = jnp.tril(jnp.ones((C, C), bool))[None, :, :, None, None]

    def step(S, inp):
        qc, kc, vc, gc = inp
        cum = jnp.cumsum(gc, axis=1)
        o_inter = jnp.einsum('bchk,bhkv->bchv', qc * jnp.exp(cum), S)
        diff = cum[:, :, None] - cum[:, None, :]
        decay = jnp.exp(jnp.where(causal, diff, -jnp.inf))
        A = jnp.einsum('bthk,bshk,btshk->bhts', qc, kc, decay)
        o_intra = jnp.einsum('bhts,bshv->bthv', A, vc)
        total = cum[:, -1]
        kdec = kc * jnp.exp(total[:, None] - cum)
        S = S * jnp.exp(total)[..., None] + jnp.einsum('bchk,bchv->bhkv', kdec, vc)
        return S, o_inter + o_intra

    S, o = lax.scan(step, S0, (chunks(q), chunks(k), chunks(v), chunks(logf)))
    return jnp.moveaxis(o, 0, 1).reshape(B, T, H, V), S


def _layer(x, pemb, pos, attend, S0, li, lb,
           pre_mix_g, w_in, lam_q1, lam_k1, lam_q2, lam_k2, attn_norm_g, hgrn_norm_g,
           w_o, post_mix_g, pre_ffn_g, w_ff_in, w_ff_out, post_ffn_g, w_pe, w_pg):
    B, T, _ = x.shape
    h = _rmsnorm(x, pre_mix_g)
    z = h @ w_in
    sizes = [DA_QK_W, DA_QK_W, DA_V_W, HG_K_W, HG_K_W, HG_V_W, HG_V_W, D_MODEL]
    idx = [int(c) for c in np.cumsum(sizes)]
    q, k, v, hq, hf, hi, hg, ga, gb = jnp.split(z, idx, axis=-1)
    q = _rope(q.reshape(B, T, DA_HEADS, 2, DA_HEAD_DIM), pos)
    k = _rope(k.reshape(B, T, DA_HEADS, 2, DA_HEAD_DIM), pos)
    v = v.reshape(B, T, DA_HEADS, DA_V_DIM)
    att = attend(q, k, v)
    lam_init = 0.8 - 0.6 * math.exp(-0.3 * li)
    lam = (jnp.exp(jnp.sum(lam_q1.astype(jnp.float32) * lam_k1.astype(jnp.float32)))
           - jnp.exp(jnp.sum(lam_q2.astype(jnp.float32) * lam_k2.astype(jnp.float32))) + lam_init)
    oa = jnp.transpose(att[:, :, 0] - lam * att[:, :, 1], (0, 2, 1, 3))
    oa = (_rmsnorm(oa, attn_norm_g) * (1.0 - lam_init)).reshape(B, T, DA_V_W)
    fq = jax.nn.silu(hq.astype(jnp.float32)).reshape(B, T, HG_HEADS, HG_K)
    f = lb + (1.0 - lb) * jax.nn.sigmoid(hf.astype(jnp.float32))
    fk = (1.0 - f).reshape(B, T, HG_HEADS, HG_K)
    logf = jnp.log(f).reshape(B, T, HG_HEADS, HG_K)
    vi = hi.astype(jnp.float32).reshape(B, T, HG_HEADS, HG_V)
    ob, S = _gla_chunked(fq, fk, vi, logf, S0.astype(jnp.float32))
    ob = _rmsnorm(ob, hgrn_norm_g) * jax.nn.silu(hg.astype(jnp.float32)).reshape(B, T, HG_HEADS, HG_V)
    ob = ob.reshape(B, T, HG_V_W)
    m = jax.nn.sigmoid(ga.astype(jnp.float32)) * oa + jax.nn.sigmoid(gb.astype(jnp.float32)) * ob
    x = x + _rmsnorm(m.astype(x.dtype) @ w_o, post_mix_g)
    g, u = jnp.split(_rmsnorm(x, pre_ffn_g) @ w_ff_in, 2, axis=-1)
    x = x + _rmsnorm((jax.nn.silu(g) * u) @ w_ff_out, post_ffn_g)
    x = x + jax.nn.sigmoid(x @ w_pg) * (pemb @ w_pe)
    return x, k, v, S


def setup_inputs(seed: int = 0) -> dict:
    key = jax.random.key(seed)
    ks = jax.random.split(key, 32)
    n_pages = PAST_LEN // PAGE_SIZE
    n_used = DEC_BATCH * n_pages
    n_pool = (5 * n_used + 3) // 4
    nrm = jax.random.normal
    f32 = jnp.float32

    def gain(k, n):
        return 1.0 + 0.02 * nrm(k, (DEPTH, n), f32)

    page_table = jax.random.permutation(ks[7], n_pool)[:n_used].reshape(DEC_BATCH, n_pages).astype(jnp.int32)
    return {
        'x_prompt': nrm(ks[0], (BATCH, SEQ, D_MODEL), f32),
        'x_sample': nrm(ks[1], (DEC_BATCH, DEC_SEQ, D_MODEL), f32),
        'cache_k': nrm(ks[2], (DEPTH, n_pool, PAGE_SIZE, DA_HEADS, 2, DA_HEAD_DIM), f32),
        'cache_v': nrm(ks[3], (DEPTH, n_pool, PAGE_SIZE, DA_HEADS, DA_V_DIM), f32),
        'state_hgrn': nrm(ks[4], (DEPTH, DEC_BATCH, HG_HEADS, HG_K, HG_V), f32),
        'page_table': page_table,
        'p_prompt': nrm(ks[5], (DEPTH, BATCH, SEQ, PLE_DIM), f32),
        'p_sample': nrm(ks[6], (DEPTH, DEC_BATCH, DEC_SEQ, PLE_DIM), f32),
        'pre_mix_g': gain(ks[8], D_MODEL),
        'w_in': nrm(ks[9], (DEPTH, D_MODEL, N_IN), f32) * D_MODEL ** -0.5,
        'lam_q1': 0.1 * nrm(ks[10], (DEPTH, DA_HEAD_DIM), f32),
        'lam_k1': 0.1 * nrm(ks[11], (DEPTH, DA_HEAD_DIM), f32),
        'lam_q2': 0.1 * nrm(ks[12], (DEPTH, DA_HEAD_DIM), f32),
        'lam_k2': 0.1 * nrm(ks[13], (DEPTH, DA_HEAD_DIM), f32),
        'attn_norm_g': gain(ks[14], DA_V_DIM),
        'hgrn_lb': 0.1 * nrm(ks[15], (DEPTH + 1, HG_K_W), f32),
        'hgrn_norm_g': gain(ks[16], HG_V),
        'w_o': nrm(ks[17], (DEPTH, D_MODEL, D_MODEL), f32) * D_MODEL ** -0.5,
        'post_mix_g': gain(ks[18], D_MODEL),
        'pre_ffn_g': gain(ks[19], D_MODEL),
        'w_ff_in': nrm(ks[20], (DEPTH, D_MODEL, 2 * D_FF), f32) * D_MODEL ** -0.5,
        'w_ff_out': nrm(ks[21], (DEPTH, D_FF, D_MODEL), f32) * D_FF ** -0.5,
        'post_ffn_g': gain(ks[22], D_MODEL),
        'w_pe': nrm(ks[23], (DEPTH, PLE_DIM, D_MODEL), f32) * PLE_DIM ** -0.5,
        'w_pg': nrm(ks[24], (DEPTH, D_MODEL, D_MODEL), f32) * D_MODEL ** -0.5,
    }


def reference(x_prompt, x_sample, cache_k, cache_v, state_hgrn, page_table, p_prompt, p_sample,
              pre_mix_g, w_in, lam_q1, lam_k1, lam_q2, lam_k2, attn_norm_g, hgrn_lb, hgrn_norm_g,
              w_o, post_mix_g, pre_ffn_g, w_ff_in, w_ff_out, post_ffn_g, w_pe, w_pg):
    lb_all = jnp.cumsum(jax.nn.softmax(hgrn_lb.astype(jnp.float32), axis=0), axis=0)
    past_len = page_table.shape[1] * PAGE_SIZE
    pos_p = jnp.arange(x_prompt.shape[1])
    pos_s = past_len + jnp.arange(x_sample.shape[1])
    s0_p = jnp.zeros((x_prompt.shape[0], HG_HEADS, HG_K, HG_V), jnp.float32)
    yp, ys = x_prompt, x_sample
    kp, vp, sp, ksm, vsm, ssm = [], [], [], [], [], []
    for i in range(DEPTH):
        lw = (pre_mix_g[i], w_in[i], lam_q1[i], lam_k1[i], lam_q2[i], lam_k2[i], attn_norm_g[i],
              hgrn_norm_g[i], w_o[i], post_mix_g[i], pre_ffn_g[i], w_ff_in[i], w_ff_out[i],
              post_ffn_g[i], w_pe[i], w_pg[i])
        yp, k_i, v_i, s_i = _layer(yp, p_prompt[i], pos_p, _diff_attn_prompt, s0_p, i, lb_all[i], *lw)
        attend_s = functools.partial(_diff_attn_sample, cache_k=cache_k, cache_v=cache_v,
                                     page_table=page_table, layer=i)
        ys, k_j, v_j, s_j = _layer(ys, p_sample[i], pos_s, attend_s, state_hgrn[i], i, lb_all[i], *lw)
        kp.append(k_i); vp.append(v_i); sp.append(s_i)
        ksm.append(k_j); vsm.append(v_j); ssm.append(s_j)
    return (yp, ys, jnp.stack(kp), jnp.stack(vp), jnp.stack(sp), jnp.stack(ksm), jnp.stack(vsm), jnp.stack(ssm))
```

```python
import functools
import math

import jax
import jax.numpy as jnp
import numpy as np
from jax import lax
from jax.experimental import pallas as pl
from jax.experimental.pallas import tpu as pltpu

BF16 = jnp.bfloat16
F32 = jnp.float32

LANES = 128
HEAD_DIM = 64
HEAD_W = 2 * HEAD_DIM
HG_W = 128
PAGE = 128
ROPE_THETA = 10000.0
EPS = 1e-6
NEG = -1e30
LOG2E = math.log2(math.e)
Q_SCALE = HEAD_DIM ** -0.5 * LOG2E
GLA_CHUNK = 64
GLA_EXP_CLAMP = 80.0
VMEM_LIMIT = 56 * 1024 * 1024


def _cparams(*sem):
    return pltpu.CompilerParams(dimension_semantics=sem, vmem_limit_bytes=VMEM_LIMIT)


def _resident(shape):
    nd = len(shape)
    return pl.BlockSpec(shape, lambda *_: (0,) * nd, pipeline_mode=pl.Buffered(1))


def _rms(x, g):
    return x * lax.rsqrt(jnp.mean(x * x, axis=-1, keepdims=True) + EPS) * g


def _sigmoid(x):
    return 1.0 / (1.0 + jnp.exp(-x))


def _qkv_kernel(x_ref, g_ref, w_ref, cos_ref, sin_ref, q_ref, k_ref, v_ref, *copies, d_model):
    h = _rms(x_ref[...], g_ref[...]).astype(BF16)
    cos, sin = cos_ref[...], sin_ref[...]
    lane = lax.broadcasted_iota(jnp.int32, cos.shape, 1)
    low_half = (lane % HEAD_DIM) < (HEAD_DIM // 2)

    def rope(z):
        zr = jnp.where(low_half, pltpu.roll(z, LANES - HEAD_DIM // 2, 1), pltpu.roll(z, HEAD_DIM // 2, 1))
        return z * cos + zr * sin

    nblk = d_model // LANES
    for c in range(nblk):
        sl = slice(c * LANES, (c + 1) * LANES)
        q = rope(jnp.dot(h, w_ref[:, sl], preferred_element_type=F32))
        q_ref[:, sl] = (q * Q_SCALE).astype(BF16)
        k = rope(jnp.dot(h, w_ref[:, d_model + c * LANES:d_model + (c + 1) * LANES], preferred_element_type=F32))
        v = jnp.dot(h, w_ref[:, 2 * d_model + c * LANES:2 * d_model + (c + 1) * LANES], preferred_element_type=F32)
        v_ref[:, sl] = v
        if copies:
            kb_ref, vt_ref = copies
            k_ref[0, 2 * c:2 * c + 2] = k.T.reshape(2, HEAD_DIM, k.shape[0])
            kb_ref[:, sl] = k.astype(BF16)
            vt_ref[0, c] = v.T.astype(BF16)
        else:
            k_ref[:, sl] = k


def _qkv_proj(x, g, w_qkv, cos, sin, *, seqs, tm, attn_copies):
    n, d = x.shape
    per_seq = n // seqs
    nt = per_seq // tm
    tok = pl.BlockSpec((tm, d), lambda b, t: (b * nt + t, 0))
    tab = pl.BlockSpec((tm, LANES), lambda b, t: (t, 0))
    out_shape = [jax.ShapeDtypeStruct((n, d), BF16), jax.ShapeDtypeStruct((n, d), F32), jax.ShapeDtypeStruct((n, d), F32)]
    out_specs = [tok, tok, tok]
    if attn_copies:
        maps = d // HEAD_DIM
        out_shape[1] = jax.ShapeDtypeStruct((seqs, maps, HEAD_DIM, per_seq), F32)
        out_specs[1] = pl.BlockSpec((1, maps, HEAD_DIM, tm), lambda b, t: (b, 0, 0, t))
        out_shape += [jax.ShapeDtypeStruct((n, d), BF16), jax.ShapeDtypeStruct((seqs, d // HEAD_W, HEAD_W, per_seq), BF16)]
        out_specs += [tok, pl.BlockSpec((1, d // HEAD_W, HEAD_W, tm), lambda b, t: (b, 0, 0, t))]
    return pl.pallas_call(
        functools.partial(_qkv_kernel, d_model=d),
        grid=(seqs, nt),
        in_specs=[tok, _resident((1, d)), _resident(w_qkv.shape), tab, tab],
        out_specs=out_specs, out_shape=out_shape,
        compiler_params=_cparams("parallel", "parallel"), name="qkv_proj",
    )(x, g, w_qkv, cos, sin)


def _hgrn_proj_kernel(x_ref, g_ref, w_ref, lb_ref, fq_ref, logf_ref, vi_ref, og_ref, *, d_model):
    h = _rms(x_ref[...], g_ref[...]).astype(BF16)
    for c in range(d_model // LANES):
        sl = slice(c * LANES, (c + 1) * LANES)

        def proj(part):
            lo = part * d_model + c * LANES
            return jnp.dot(h, w_ref[:, lo:lo + LANES], preferred_element_type=F32)

        hq, hf, hi, hg = proj(0), proj(1), proj(2), proj(3)
        lb = lb_ref[:, sl]
        fq_ref[:, sl] = (hq * _sigmoid(hq)).astype(BF16)
        logf_ref[:, sl] = jnp.log(lb + (1.0 - lb) * _sigmoid(hf))
        vi_ref[:, sl] = hi.astype(BF16)
        og_ref[:, sl] = (hg * _sigmoid(hg)).astype(BF16)


def _hgrn_proj(x, g, w_h, lb, *, tm):
    n, d = x.shape
    tok = pl.BlockSpec((tm, d), lambda t: (t, 0))
    return pl.pallas_call(
        functools.partial(_hgrn_proj_kernel, d_model=d),
        grid=(n // tm,),
        in_specs=[tok, _resident((1, d)), _resident(w_h.shape), _resident((1, d))],
        out_specs=[tok, tok, tok, tok],
        out_shape=[jax.ShapeDtypeStruct((n, d), BF16), jax.ShapeDtypeStruct((n, d), F32),
                   jax.ShapeDtypeStruct((n, d), BF16), jax.ShapeDtypeStruct((n, d), BF16)],
        compiler_params=_cparams("parallel"), name="hgrn_proj",
    )(x, g, w_h, lb)


def _chunk_cumsum(g, chunk):
    row = lax.broadcasted_iota(jnp.int32, g.shape, 0) % chunk
    cum, sh = g, 1
    while sh < chunk:
        cum = cum + jnp.where(row >= sh, pltpu.roll(cum, sh, 0), 0.0)
        sh *= 2
    return cum


def _gla_kernel(fq_ref, logf_ref, vi_ref, og_ref, s0_ref, gn_ref, ob_ref, s_out_ref, st_ref, *, chunk, heads):
    t = pl.program_id(1)

    @pl.when(t == 0)
    def _():
        for h in range(heads):
            st_ref[h] = s0_ref[0, h].T

    tt = fq_ref.shape[1]
    tri = lax.broadcasted_iota(jnp.int32, (chunk, chunk), 0) >= lax.broadcasted_iota(jnp.int32, (chunk, chunk), 1)
    gn = gn_ref[...]

    def chunk_body(c, carry):
        rows = pl.ds(pl.multiple_of(c * chunk, chunk), chunk)
        g = logf_ref[0, rows, :]
        cum = _chunk_cumsum(g, chunk)
        total = cum[chunk - 1:chunk, :]
        fk = 1.0 - jnp.exp(g)
        q_in = (fq_ref[0, rows, :].astype(F32) * jnp.exp(cum)).astype(BF16)
        k_out = (fk * jnp.exp(jnp.minimum(-cum, GLA_EXP_CLAMP))).astype(BF16)
        k_end = (fk * jnp.exp(total - cum)).astype(BF16)
        dec = jnp.exp(total)
        v = vi_ref[0, rows, :]
        for h in range(heads):
            sl = slice(h * HG_W, (h + 1) * HG_W)
            st = st_ref[h]
            o = lax.dot_general(q_in[:, sl], st.astype(BF16), (((1,), (1,)), ((), ())), preferred_element_type=F32)
            a = lax.dot_general(q_in[:, sl], k_out[:, sl], (((1,), (1,)), ((), ())), preferred_element_type=F32)
            a = jnp.where(tri, a, 0.0).astype(BF16)
            o = o + jnp.dot(a, v[:, sl], preferred_element_type=F32)
            st_ref[h] = st * dec[:, sl] + lax.dot_general(v[:, sl], k_end[:, sl], (((0,), (0,)), ((), ())),
                                                          preferred_element_type=F32)
            o = _rms(o, gn) * og_ref[0, rows, sl].astype(F32)
            ob_ref[0, rows, sl] = o.astype(BF16)
        return carry

    lax.fori_loop(0, tt // chunk, chunk_body, 0)

    @pl.when(t == pl.num_programs(1) - 1)
    def _():
        for h in range(heads):
            s_out_ref[0, h] = st_ref[h].T


def _gla(fq, logf, vi, og, s0, gn, *, tt, chunk):
    b, t, w = fq.shape
    heads = w // HG_W
    tok = pl.BlockSpec((1, tt, w), lambda i, j: (i, j, 0))
    st = pl.BlockSpec((1, heads, HG_W, HG_W), lambda i, j: (i, 0, 0, 0))
    return pl.pallas_call(
        functools.partial(_gla_kernel, chunk=chunk, heads=heads),
        grid=(b, t // tt),
        in_specs=[tok, tok, tok, tok, st, _resident((1, HG_W))],
        out_specs=[tok, st],
        out_shape=[jax.ShapeDtypeStruct((b, t, w), BF16), jax.ShapeDtypeStruct(s0.shape, F32)],
        scratch_shapes=[pltpu.VMEM((heads, HG_W, HG_W), F32)],
        compiler_params=_cparams("parallel", "arbitrary"), name="gla",
    )(fq, logf, vi, og, s0, gn)


def _diff_norm(o0, o1, lam, gn, out_scale, axis):
    o = o0 - lam * o1
    return o * lax.rsqrt(jnp.mean(o * o, axis=axis, keepdims=True) + EPS) * gn * out_scale


def _prompt_attn_kernel(q_ref, k_ref, vt_ref, lam_ref, gn_ref, o_ref, m_ref, l_ref, acc_ref, *, tk, out_scale):
    qi = pl.program_id(2)
    tq = q_ref.shape[1]
    q = q_ref[0]
    lane = lax.broadcasted_iota(jnp.int32, q.shape, 1)
    zero = jnp.zeros_like(q)
    qm = (jnp.where(lane < HEAD_DIM, q, zero), jnp.where(lane >= HEAD_DIM, q, zero))
    m_ref[...] = jnp.full(m_ref.shape, NEG, F32)
    l_ref[...] = jnp.zeros(l_ref.shape, F32)
    acc_ref[...] = jnp.zeros(acc_ref.shape, F32)

    def step(j, mask):
        rows = pl.ds(pl.multiple_of(j * tk, tk), tk)
        kt = k_ref[0, rows, :]
        vt = vt_ref[0, 0, :, rows]
        for mp in range(2):
            s = lax.dot_general(kt, qm[mp], (((1,), (1,)), ((), ())), preferred_element_type=F32)
            if mask is not None:
                s = jnp.where(mask, s, NEG)
            m_old = m_ref[mp]
            m_new = jnp.maximum(m_old, jnp.max(s, axis=0, keepdims=True))
            p = jnp.exp2(s - m_new)
            if mask is not None:
                p = jnp.where(mask, p, 0.0)
            alpha = jnp.exp2(m_old - m_new)
            l_ref[mp] = l_ref[mp] * alpha + jnp.sum(p, axis=0, keepdims=True)
            acc_ref[mp] = acc_ref[mp] * alpha + jnp.dot(vt, p.astype(BF16), preferred_element_type=F32)
            m_ref[mp] = m_new

    def body(j, carry):
        step(j, None)
        return carry

    nfull = qi * (tq // tk)
    lax.fori_loop(0, nfull, body, 0)
    kpos = lax.broadcasted_iota(jnp.int32, (tk, tq), 0)
    qpos = lax.broadcasted_iota(jnp.int32, (tk, tq), 1)
    for d in range(tq // tk):
        step(nfull + d, kpos + d * tk <= qpos)

    o = _diff_norm(acc_ref[0] / l_ref[0], acc_ref[1] / l_ref[1], lam_ref[...], gn_ref[...], out_scale, 0)
    o_ref[0] = o.T.astype(BF16)


def _prompt_attn(q, kb, vt, lam_col, gn_col, *, tq, tk, out_scale):
    b, s, w = q.shape
    heads = w // HEAD_W
    return pl.pallas_call(
        functools.partial(_prompt_attn_kernel, tk=tk, out_scale=out_scale),
        grid=(b, heads, s // tq),
        in_specs=[pl.BlockSpec((1, tq, HEAD_W), lambda i, h, j: (i, j, h)),
                  pl.BlockSpec((1, s, HEAD_W), lambda i, h, j: (i, 0, h)),
                  pl.BlockSpec((1, 1, HEAD_W, s), lambda i, h, j: (i, h, 0, 0)),
                  _resident((HEAD_W, 1)), _resident((HEAD_W, 1))],
        out_specs=pl.BlockSpec((1, tq, HEAD_W), lambda i, h, j: (i, j, h)),
        out_shape=jax.ShapeDtypeStruct((b, s, w), BF16),
        scratch_shapes=[pltpu.VMEM((2, 1, tq), F32), pltpu.VMEM((2, 1, tq), F32), pltpu.VMEM((2, HEAD_W, tq), F32)],
        compiler_params=_cparams("parallel", "parallel", "arbitrary"), name="prompt_attn",
    )(q, kb, vt, lam_col, gn_col)


def _sample_attn_kernel(pt_ref, qbd_ref, *refs, pages, t_new, heads, out_scale):
    k_refs, v_refs = refs[:pages], refs[pages:2 * pages]
    kn_ref, vn_ref, lam_ref, gn_ref, o_ref, m_ref, l_ref, acc_ref = refs[2 * pages:]
    pg = pl.program_id(1)

    @pl.when(pg == 0)
    def _():
        m_ref[...] = jnp.full(m_ref.shape, NEG, F32)
        l_ref[...] = jnp.zeros(l_ref.shape, F32)
        acc_ref[...] = jnp.zeros(acc_ref.shape, F32)

    qbd = qbd_ref[0]

    def update(s, v, mask):
        if mask is not None:
            s = jnp.where(mask, s, NEG)
        m_old = m_ref[...]
        m_new = jnp.maximum(m_old, jnp.max(s, axis=1, keepdims=True))
        p = jnp.exp2(s - m_new)
        if mask is not None:
            p = jnp.where(mask, p, 0.0)
        alpha = jnp.exp2(m_old - m_new)
        l_ref[...] = l_ref[...] * alpha + jnp.sum(p, axis=1, keepdims=True)
        acc_ref[...] = acc_ref[...] * alpha + jnp.dot(p.astype(BF16), v, preferred_element_type=F32)
        m_ref[...] = m_new

    w = qbd.shape[1]
    kt = jnp.concatenate([r[0, 0].reshape(w, PAGE).astype(BF16) for r in k_refs], axis=1)
    v = jnp.concatenate(
        [jnp.concatenate([r[0, 0, :, h, :] for h in range(heads)], axis=1).astype(BF16) for r in v_refs], axis=0)
    update(jnp.dot(qbd, kt, preferred_element_type=F32), v, None)

    @pl.when(pg == pl.num_programs(1) - 1)
    def _():
        rows = qbd.shape[0]
        tpad = kn_ref.shape[1]
        tq = lax.broadcasted_iota(jnp.int32, (rows, tpad), 0) % t_new
        tk = lax.broadcasted_iota(jnp.int32, (rows, tpad), 1)
        s_new = lax.dot_general(qbd, kn_ref[0].astype(BF16), (((1,), (1,)), ((), ())), preferred_element_type=F32)
        update(s_new, vn_ref[0].astype(BF16), tk <= tq)
        out = acc_ref[...] / l_ref[...]
        lam, gn = lam_ref[...], gn_ref[...]
        for h in range(heads):
            blk = out[h * 2 * t_new:(h + 1) * 2 * t_new, h * HEAD_W:(h + 1) * HEAD_W]
            o_ref[0, :, h * HEAD_W:(h + 1) * HEAD_W] = _diff_norm(blk[:t_new], blk[t_new:], lam, gn, out_scale, 1)


def _sample_attn(page_table, qbd, cache_k, cache_v, k_new, v_new, lam_row, gn_row, *, layer, pages, t_new, out_scale):
    b, rows, w = qbd.shape
    heads = w // HEAD_W
    n_pages = page_table.shape[1]
    tpad = k_new.shape[1]

    def k_spec(i):
        return pl.BlockSpec((1, 1, heads, 2, HEAD_DIM, PAGE), lambda bi, pg, pt: (layer, pt[bi, pg * pages + i], 0, 0, 0, 0))

    def v_spec(i):
        return pl.BlockSpec((1, 1, PAGE, heads, HEAD_W), lambda bi, pg, pt: (layer, pt[bi, pg * pages + i], 0, 0, 0))

    per_b = lambda shape: pl.BlockSpec(shape, lambda bi, pg, pt: (bi, 0, 0))
    const = lambda shape: pl.BlockSpec(shape, lambda bi, pg, pt: (0, 0))
    grid_spec = pltpu.PrefetchScalarGridSpec(
        num_scalar_prefetch=1, grid=(b, n_pages // pages),
        in_specs=[per_b((1, rows, w))] + [k_spec(i) for i in range(pages)] + [v_spec(i) for i in range(pages)]
                 + [per_b((1, tpad, w)), per_b((1, tpad, w)), const((1, HEAD_W)), const((1, HEAD_W))],
        out_specs=per_b((1, t_new, w)),
        scratch_shapes=[pltpu.VMEM((rows, 1), F32), pltpu.VMEM((rows, 1), F32), pltpu.VMEM((rows, w), F32)])
    return pl.pallas_call(
        functools.partial(_sample_attn_kernel, pages=pages, t_new=t_new, heads=heads, out_scale=out_scale),
        grid_spec=grid_spec,
        out_shape=jax.ShapeDtypeStruct((b, t_new, w), F32),
        compiler_params=_cparams("parallel", "arbitrary"), name="sample_attn",
    )(page_table, qbd, *([cache_k] * pages), *([cache_v] * pages), k_new, v_new, lam_row, gn_row)


def _mix_ffn_kernel(x_ref, oa_ref, ob_ref, pe_ref, g_pre_ref, w_gate_ref, w_o_ref, g_postmix_ref, g_preffn_ref,
                    w_ff_in_ref, w_ff_out_ref, g_postffn_ref, w_pe_ref, w_pg_ref, y_ref, *, d_model, d_ff):
    x = x_ref[...]
    h = _rms(x, g_pre_ref[...]).astype(BF16)
    gates = _sigmoid(jnp.dot(h, w_gate_ref[...], preferred_element_type=F32))
    m = gates[:, :d_model] * oa_ref[...].astype(F32) + gates[:, d_model:] * ob_ref[...].astype(F32)
    x = x + _rms(jnp.dot(m.astype(BF16), w_o_ref[...], preferred_element_type=F32), g_postmix_ref[...])
    h = _rms(x, g_preffn_ref[...]).astype(BF16)
    g = jnp.dot(h, w_ff_in_ref[:, :d_ff], preferred_element_type=F32)
    u = jnp.dot(h, w_ff_in_ref[:, d_ff:], preferred_element_type=F32)
    act = (g * _sigmoid(g) * u).astype(BF16)
    x = x + _rms(jnp.dot(act, w_ff_out_ref[...], preferred_element_type=F32), g_postffn_ref[...])
    pg = _sigmoid(jnp.dot(x.astype(BF16), w_pg_ref[...], preferred_element_type=F32))
    y_ref[...] = x + pg * jnp.dot(pe_ref[...].astype(BF16), w_pe_ref[...], preferred_element_type=F32)


def _mix_ffn(x, oa, ob, pemb, g_pre, w_gate, w_o, g_postmix, g_preffn, w_ff_in, w_ff_out, g_postffn, w_pe, w_pg, *, tm):
    n, d = x.shape
    d_ff = w_ff_out.shape[0]
    tok = lambda w: pl.BlockSpec((tm, w), lambda t: (t, 0))
    weights = (g_pre, w_gate, w_o, g_postmix, g_preffn, w_ff_in, w_ff_out, g_postffn, w_pe, w_pg)
    return pl.pallas_call(
        functools.partial(_mix_ffn_kernel, d_model=d, d_ff=d_ff),
        grid=(n // tm,),
        in_specs=[tok(d), tok(d), tok(d), tok(pemb.shape[1])] + [_resident(a.shape) for a in weights],
        out_specs=tok(d), out_shape=jax.ShapeDtypeStruct((n, d), F32),
        compiler_params=_cparams("parallel"), name="mix_ffn",
    )(x, oa, ob, pemb, *weights)


def _rope_tables(pos):
    half = HEAD_DIM // 2
    inv = 1.0 / (ROPE_THETA ** (jnp.arange(half, dtype=F32) / half))
    ang = pos.astype(F32)[:, None] * inv[None, :]
    cos, sin = jnp.cos(ang), jnp.sin(ang)
    return jnp.tile(jnp.concatenate([cos, cos], -1), (1, 2)), jnp.tile(jnp.concatenate([-sin, sin], -1), (1, 2))


def _tile(n, cap):
    t = min(n, cap)
    assert n % t == 0, (n, cap)
    return t


def kernel(x_prompt, x_sample, cache_k, cache_v, state_hgrn, page_table, p_prompt, p_sample, pre_mix_g, w_in, lam_q1, lam_k1, lam_q2, lam_k2, attn_norm_g, hgrn_lb, hgrn_norm_g, w_o, post_mix_g, pre_ffn_g, w_ff_in, w_ff_out, post_ffn_g, w_pe, w_pg):
    depth = w_in.shape[0]
    bp, sp, d = x_prompt.shape
    bs, ts, _ = x_sample.shape
    heads = d // HEAD_W
    n_pages = page_table.shape[1]
    assert d % HEAD_W == 0 and cache_k.shape[2] == PAGE
    lb_all = jnp.cumsum(jax.nn.softmax(hgrn_lb.astype(F32), axis=0), axis=0)
    cos_p, sin_p = _rope_tables(jnp.arange(sp))
    cos_s, sin_s = _rope_tables(jnp.tile(n_pages * PAGE + jnp.arange(ts), bs))
    ck = jnp.transpose(cache_k, (0, 1, 3, 4, 5, 2))
    cv = cache_v
    ts_pad = -(-ts // 8) * 8
    pad_t = lambda a: jnp.pad(a.reshape(bs, ts, d), ((0, 0), (0, ts_pad - ts), (0, 0)))
    eye = jnp.eye(2 * heads, dtype=BF16)

    yp, ys = x_prompt.reshape(bp * sp, d), x_sample.reshape(bs * ts, d)
    outs = [[] for _ in range(6)]
    for i in range(depth):
        row = lambda a: a[i].reshape(1, -1).astype(F32)
        wi = w_in[i].astype(BF16)
        w_qkv, w_h, w_gate = wi[:, :3 * d], wi[:, 3 * d:7 * d], wi[:, 7 * d:]
        lam_init = 0.8 - 0.6 * math.exp(-0.3 * i)
        lam = (jnp.exp(jnp.sum(lam_q1[i].astype(F32) * lam_k1[i].astype(F32)))
               - jnp.exp(jnp.sum(lam_q2[i].astype(F32) * lam_k2[i].astype(F32))) + lam_init)
        lam_row = jnp.full((1, HEAD_W), lam, F32)
        gn_row = row(attn_norm_g)
        lb = lb_all[i].reshape(1, d)
        mix_w = (row(pre_mix_g), w_gate, w_o[i].astype(BF16), row(post_mix_g), row(pre_ffn_g), w_ff_in[i].astype(BF16),
                 w_ff_out[i].astype(BF16), row(post_ffn_g), w_pe[i].astype(BF16), w_pg[i].astype(BF16))

        q, k, v, kb, vt = _qkv_proj(yp, row(pre_mix_g), w_qkv, cos_p, sin_p, seqs=bp, tm=_tile(sp, 512), attn_copies=True)
        oa = _prompt_attn(q.reshape(bp, sp, d), kb.reshape(bp, sp, d), vt, lam_row.reshape(HEAD_W, 1),
                          gn_row.reshape(HEAD_W, 1), tq=_tile(sp, 512), tk=_tile(sp, 512), out_scale=1.0 - lam_init)
        fq, logf, vi, og = _hgrn_proj(yp, row(pre_mix_g), w_h, lb, tm=_tile(bp * sp, 512))
        r3 = lambda a: a.reshape(bp, sp, d)
        ob, s_p = _gla(r3(fq), r3(logf), r3(vi), r3(og), jnp.zeros((bp, d // HG_W, HG_W, HG_W), F32),
                       row(hgrn_norm_g), tt=_tile(sp, 256), chunk=GLA_CHUNK)
        yp = _mix_ffn(yp, oa.reshape(bp * sp, d), ob.reshape(bp * sp, d), p_prompt[i].reshape(bp * sp, -1), *mix_w,
                      tm=_tile(bp * sp, 256))
        k = jnp.transpose(k.reshape(bp, heads, 2, HEAD_DIM, sp), (0, 4, 1, 2, 3))
        outs[0].append(k); outs[1].append(v.reshape(bp, sp, heads, HEAD_W)); outs[2].append(s_p)

        q, k, v = _qkv_proj(ys, row(pre_mix_g), w_qkv, cos_s, sin_s, seqs=1, tm=_tile(bs * ts, 512), attn_copies=False)
        qt = q.reshape(bs, ts, 2 * heads, HEAD_DIM).transpose(0, 2, 1, 3)
        qbd = (qt[:, :, :, None, :] * eye[None, :, None, :, None]).reshape(bs, 2 * heads * ts, d)
        oa = _sample_attn(page_table, qbd, ck, cv, pad_t(k), pad_t(v), lam_row, gn_row, layer=i,
                          pages=math.gcd(n_pages, 4), t_new=ts, out_scale=1.0 - lam_init)
        fq, logf, vi, og = _hgrn_proj(ys, row(pre_mix_g), w_h, lb, tm=_tile(bs * ts, 512))
        ob, s_s = _gla(pad_t(fq), pad_t(logf), pad_t(vi), pad_t(og), state_hgrn[i].astype(F32), row(hgrn_norm_g),
                       tt=ts_pad, chunk=ts_pad)
        ys = _mix_ffn(ys, oa.reshape(bs * ts, d).astype(BF16), ob[:, :ts].reshape(bs * ts, d), p_sample[i].reshape(bs * ts, -1),
                      *mix_w, tm=_tile(bs * ts, 256))
        outs[3].append(k.reshape(bs, ts, heads, 2, HEAD_DIM)); outs[4].append(v.reshape(bs, ts, heads, HEAD_W)); outs[5].append(s_s)

    st = [jnp.stack(o) for o in outs]
    return (yp.reshape(bp, sp, d), ys.reshape(bs, ts, d), st[0], st[1], st[2], st[3], st[4], st[5])
```

```python
import functools
import math

import jax
import jax.numpy as jnp
import numpy as np
from jax import lax
from jax.experimental import pallas as pl
from jax.experimental.pallas import tpu as pltpu

BF16 = jnp.bfloat16
F32 = jnp.float32

LANES = 128
MXU_COLS = 256
HEAD_DIM = 64
HEAD_W = 2 * HEAD_DIM
HG_W = 128
PAGE = 128
ROPE_THETA = 10000.0
EPS = 1e-6
NEG = -1e30
LOG2E = math.log2(math.e)
Q_SCALE = HEAD_DIM ** -0.5 * LOG2E
GLA_CHUNK = 64
GLA_EXP_CLAMP = 80.0
VMEM_LIMIT = 56 * 1024 * 1024


def _cparams(*sem):
    return pltpu.CompilerParams(dimension_semantics=sem, vmem_limit_bytes=VMEM_LIMIT)


def _resident(shape):
    nd = len(shape)
    return pl.BlockSpec(shape, lambda *_: (0,) * nd, pipeline_mode=pl.Buffered(1))


def _rms(x, g):
    return x * lax.rsqrt(jnp.mean(x * x, axis=-1, keepdims=True) + EPS) * g


def _sigmoid(x):
    return 1.0 / (1.0 + jnp.exp(-x))


def _qkv_kernel(x_ref, g_ref, w_ref, cos_ref, sin_ref, q_ref, k_ref, v_ref, *copies, d_model):
    h = _rms(x_ref[...], g_ref[...]).astype(BF16)
    cos, sin = cos_ref[...], sin_ref[...]
    lane = lax.broadcasted_iota(jnp.int32, cos.shape, 1)
    low_half = (lane % HEAD_DIM) < (HEAD_DIM // 2)

    def rope(z):
        zr = jnp.where(low_half, pltpu.roll(z, LANES - HEAD_DIM // 2, 1), pltpu.roll(z, HEAD_DIM // 2, 1))
        return z * cos + zr * sin

    wide = {}
    for c in range(d_model // LANES):
        sl = slice(c * LANES, (c + 1) * LANES)
        blk, sub = divmod(c * LANES, MXU_COLS)
        if sub == 0:
            for part in range(3):
                lo = part * d_model + blk * MXU_COLS
                wide[part] = jnp.dot(h, w_ref[:, lo:lo + MXU_COLS], preferred_element_type=F32)
        q, k, v = (wide[part][:, sub:sub + LANES] for part in range(3))
        q_ref[:, sl] = (rope(q) * Q_SCALE).astype(BF16)
        k = rope(k)
        v_ref[:, sl] = v
        if copies:
            kb_ref, vt_ref = copies
            k_ref[0, 2 * c:2 * c + 2] = k.T.reshape(2, HEAD_DIM, k.shape[0])
            kb_ref[:, sl] = k.astype(BF16)
            vt_ref[0, c] = v.T.astype(BF16)
        else:
            k_ref[:, sl] = k


def _qkv_proj(x, g, w_qkv, cos, sin, *, seqs, tm, attn_copies):
    n, d = x.shape
    per_seq = n // seqs
    nt = per_seq // tm
    tok = pl.BlockSpec((tm, d), lambda b, t: (b * nt + t, 0))
    tab = pl.BlockSpec((tm, LANES), lambda b, t: (t, 0))
    out_shape = [jax.ShapeDtypeStruct((n, d), BF16), jax.ShapeDtypeStruct((n, d), F32), jax.ShapeDtypeStruct((n, d), F32)]
    out_specs = [tok, tok, tok]
    if attn_copies:
        maps = d // HEAD_DIM
        out_shape[1] = jax.ShapeDtypeStruct((seqs, maps, HEAD_DIM, per_seq), F32)
        out_specs[1] = pl.BlockSpec((1, maps, HEAD_DIM, tm), lambda b, t: (b, 0, 0, t))
        out_shape += [jax.ShapeDtypeStruct((n, d), BF16), jax.ShapeDtypeStruct((seqs, d // HEAD_W, HEAD_W, per_seq), BF16)]
        out_specs += [tok, pl.BlockSpec((1, d // HEAD_W, HEAD_W, tm), lambda b, t: (b, 0, 0, t))]
    return pl.pallas_call(
        functools.partial(_qkv_kernel, d_model=d),
        grid=(seqs, nt),
        in_specs=[tok, _resident((1, d)), _resident(w_qkv.shape), tab, tab],
        out_specs=out_specs, out_shape=out_shape,
        compiler_params=_cparams("parallel", "parallel"), name="qkv_proj",
    )(x, g, w_qkv, cos, sin)


def _hgrn_proj_kernel(x_ref, g_ref, w_ref, lb_ref, fq_ref, logf_ref, vi_ref, og_ref, *, d_model):
    h = _rms(x_ref[...], g_ref[...]).astype(BF16)
    for c in range(d_model // MXU_COLS):
        sl = slice(c * MXU_COLS, (c + 1) * MXU_COLS)

        def proj(part):
            lo = part * d_model + c * MXU_COLS
            return jnp.dot(h, w_ref[:, lo:lo + MXU_COLS], preferred_element_type=F32)

        hq, hf, hi, hg = proj(0), proj(1), proj(2), proj(3)
        lb = lb_ref[:, sl]
        fq_ref[:, sl] = (hq * _sigmoid(hq)).astype(BF16)
        logf_ref[:, sl] = jnp.log(lb + (1.0 - lb) * _sigmoid(hf))
        vi_ref[:, sl] = hi.astype(BF16)
        og_ref[:, sl] = (hg * _sigmoid(hg)).astype(BF16)


def _hgrn_proj(x, g, w_h, lb, *, tm):
    n, d = x.shape
    tok = pl.BlockSpec((tm, d), lambda t: (t, 0))
    return pl.pallas_call(
        functools.partial(_hgrn_proj_kernel, d_model=d),
        grid=(n // tm,),
        in_specs=[tok, _resident((1, d)), _resident(w_h.shape), _resident((1, d))],
        out_specs=[tok, tok, tok, tok],
        out_shape=[jax.ShapeDtypeStruct((n, d), BF16), jax.ShapeDtypeStruct((n, d), F32),
                   jax.ShapeDtypeStruct((n, d), BF16), jax.ShapeDtypeStruct((n, d), BF16)],
        compiler_params=_cparams("parallel"), name="hgrn_proj",
    )(x, g, w_h, lb)


def _chunk_cumsum(g, chunk):
    row = lax.broadcasted_iota(jnp.int32, g.shape, 0) % chunk
    cum, sh = g, 1
    while sh < chunk:
        cum = cum + jnp.where(row >= sh, pltpu.roll(cum, sh, 0), 0.0)
        sh *= 2
    return cum


def _gla_kernel(fq_ref, logf_ref, vi_ref, og_ref, s0_ref, gn_ref, ob_ref, s_out_ref, st_ref, *, chunk, heads):
    t = pl.program_id(1)

    @pl.when(t == 0)
    def _():
        for h in range(heads):
            st_ref[h] = s0_ref[0, h].T

    tt = fq_ref.shape[1]
    tri = lax.broadcasted_iota(jnp.int32, (chunk, chunk), 0) >= lax.broadcasted_iota(jnp.int32, (chunk, chunk), 1)
    gn = gn_ref[...]

    def chunk_body(c, carry):
        rows = pl.ds(pl.multiple_of(c * chunk, chunk), chunk)
        g = logf_ref[0, rows, :]
        cum = _chunk_cumsum(g, chunk)
        total = cum[chunk - 1:chunk, :]
        fk = 1.0 - jnp.exp(g)
        q_in = (fq_ref[0, rows, :].astype(F32) * jnp.exp(cum)).astype(BF16)
        k_out = (fk * jnp.exp(jnp.minimum(-cum, GLA_EXP_CLAMP))).astype(BF16)
        k_end = (fk * jnp.exp(total - cum)).astype(BF16)
        dec = jnp.exp(total)
        v = vi_ref[0, rows, :]
        for h in range(heads):
            sl = slice(h * HG_W, (h + 1) * HG_W)
            st = st_ref[h]
            o = lax.dot_general(q_in[:, sl], st.astype(BF16), (((1,), (1,)), ((), ())), preferred_element_type=F32)
            a = lax.dot_general(q_in[:, sl], k_out[:, sl], (((1,), (1,)), ((), ())), preferred_element_type=F32)
            a = jnp.where(tri, a, 0.0).astype(BF16)
            o = o + jnp.dot(a, v[:, sl], preferred_element_type=F32)
            st_ref[h] = st * dec[:, sl] + lax.dot_general(v[:, sl], k_end[:, sl], (((0,), (0,)), ((), ())),
                                                          preferred_element_type=F32)
            o = _rms(o, gn) * og_ref[0, rows, sl].astype(F32)
            ob_ref[0, rows, sl] = o.astype(BF16)
        return carry

    lax.fori_loop(0, tt // chunk, chunk_body, 0)

    @pl.when(t == pl.num_programs(1) - 1)
    def _():
        for h in range(heads):
            s_out_ref[0, h] = st_ref[h].T


def _gla(fq, logf, vi, og, s0, gn, *, tt, chunk):
    b, t, w = fq.shape
    heads = w // HG_W
    tok = pl.BlockSpec((1, tt, w), lambda i, j: (i, j, 0))
    st = pl.BlockSpec((1, heads, HG_W, HG_W), lambda i, j: (i, 0, 0, 0))
    return pl.pallas_call(
        functools.partial(_gla_kernel, chunk=chunk, heads=heads),
        grid=(b, t // tt),
        in_specs=[tok, tok, tok, tok, st, _resident((1, HG_W))],
        out_specs=[tok, st],
        out_shape=[jax.ShapeDtypeStruct((b, t, w), BF16), jax.ShapeDtypeStruct(s0.shape, F32)],
        scratch_shapes=[pltpu.VMEM((heads, HG_W, HG_W), F32)],
        compiler_params=_cparams("parallel", "arbitrary"), name="gla",
    )(fq, logf, vi, og, s0, gn)


def _diff_norm(o0, o1, lam, gn, out_scale, axis):
    o = o0 - lam * o1
    return o * lax.rsqrt(jnp.mean(o * o, axis=axis, keepdims=True) + EPS) * gn * out_scale


def _prompt_attn_kernel(q_ref, k_ref, vt_ref, lam_ref, gn_ref, o_ref, m_ref, l_ref, acc_ref, s0_ref, s1_ref, *, out_scale):
    qi = pl.program_id(2)
    tq = q_ref.shape[1]
    tk = tq
    q = q_ref[0]
    lane = lax.broadcasted_iota(jnp.int32, q.shape, 1)
    zero = jnp.zeros_like(q)
    qm = (jnp.where(lane < HEAD_DIM, q, zero), jnp.where(lane >= HEAD_DIM, q, zero))
    s_refs = (s0_ref, s1_ref)
    m_ref[...] = jnp.full(m_ref.shape, NEG, F32)
    l_ref[...] = jnp.zeros(l_ref.shape, F32)
    acc_ref[...] = jnp.zeros(acc_ref.shape, F32)

    def rows(j):
        return pl.ds(pl.multiple_of(j * tk, tk), tk)

    def scores(j, mp):
        s_refs[mp][...] = lax.dot_general(k_ref[0, rows(j), :], qm[mp], (((1,), (1,)), ((), ())),
                                          preferred_element_type=F32)

    def consume(j, mp, mask):
        s = s_refs[mp][...]
        if mask is not None:
            s = jnp.where(mask, s, NEG)
        m_old = m_ref[mp]
        m_new = jnp.maximum(m_old, jnp.max(s, axis=0, keepdims=True))
        p = jnp.exp2(s - m_new)
        if mask is not None:
            p = jnp.where(mask, p, 0.0)
        alpha = jnp.exp2(m_old - m_new)
        l_ref[mp] = l_ref[mp] * alpha + jnp.sum(p, axis=0, keepdims=True)
        acc_ref[mp] = acc_ref[mp] * alpha + jnp.dot(vt_ref[0, 0, :, rows(j)], p.astype(BF16), preferred_element_type=F32)
        m_ref[mp] = m_new

    def full_tile(j):
        scores(j, 1)
        consume(j, 0, None)
        scores(j + 1, 0)
        consume(j, 1, None)

    def pair(i, carry):
        full_tile(odd + 2 * i)
        full_tile(odd + 2 * i + 1)
        return carry

    odd = qi % 2
    scores(0, 0)
    pl.when(odd == 1)(lambda: full_tile(0))
    lax.fori_loop(0, qi // 2, pair, 0)
    scores(qi, 1)
    causal = lax.broadcasted_iota(jnp.int32, (tk, tq), 0) <= lax.broadcasted_iota(jnp.int32, (tk, tq), 1)
    consume(qi, 0, causal)
    consume(qi, 1, causal)

    o = _diff_norm(acc_ref[0] / l_ref[0], acc_ref[1] / l_ref[1], lam_ref[...], gn_ref[...], out_scale, 0)
    o_ref[0] = o.T.astype(BF16)


def _prompt_attn(q, kb, vt, lam_col, gn_col, *, tq, out_scale):
    b, s, w = q.shape
    heads = w // HEAD_W
    return pl.pallas_call(
        functools.partial(_prompt_attn_kernel, out_scale=out_scale),
        grid=(b, heads, s // tq),
        in_specs=[pl.BlockSpec((1, tq, HEAD_W), lambda i, h, j: (i, j, h)),
                  pl.BlockSpec((1, s, HEAD_W), lambda i, h, j: (i, 0, h)),
                  pl.BlockSpec((1, 1, HEAD_W, s), lambda i, h, j: (i, h, 0, 0)),
                  _resident((HEAD_W, 1)), _resident((HEAD_W, 1))],
        out_specs=pl.BlockSpec((1, tq, HEAD_W), lambda i, h, j: (i, j, h)),
        out_shape=jax.ShapeDtypeStruct((b, s, w), BF16),
        scratch_shapes=[pltpu.VMEM((2, 1, tq), F32), pltpu.VMEM((2, 1, tq), F32), pltpu.VMEM((2, HEAD_W, tq), F32),
                        pltpu.VMEM((tq, tq), F32), pltpu.VMEM((tq, tq), F32)],
        compiler_params=_cparams("parallel", "parallel", "arbitrary"), name="prompt_attn",
    )(q, kb, vt, lam_col, gn_col)


def _sample_attn_kernel(pt_ref, qbd_ref, *refs, pages, t_new, heads, out_scale):
    k_refs, v_refs = refs[:pages], refs[pages:2 * pages]
    kn_ref, vn_ref, lam_ref, gn_ref, o_ref, m_ref, l_ref, acc_ref = refs[2 * pages:]
    pg = pl.program_id(1)

    @pl.when(pg == 0)
    def _():
        m_ref[...] = jnp.full(m_ref.shape, NEG, F32)
        l_ref[...] = jnp.zeros(l_ref.shape, F32)
        acc_ref[...] = jnp.zeros(acc_ref.shape, F32)

    qbd = qbd_ref[0]

    def update(s, v, mask):
        if mask is not None:
            s = jnp.where(mask, s, NEG)
        m_old = m_ref[...]
        m_new = jnp.maximum(m_old, jnp.max(s, axis=1, keepdims=True))
        p = jnp.exp2(s - m_new)
        if mask is not None:
            p = jnp.where(mask, p, 0.0)
        alpha = jnp.exp2(m_old - m_new)
        l_ref[...] = l_ref[...] * alpha + jnp.sum(p, axis=1, keepdims=True)
        acc_ref[...] = acc_ref[...] * alpha + jnp.dot(p.astype(BF16), v, preferred_element_type=F32)
        m_ref[...] = m_new

    w = qbd.shape[1]
    kt = jnp.concatenate([r[0, 0].reshape(w, PAGE).astype(BF16) for r in k_refs], axis=1)
    v = jnp.concatenate(
        [jnp.concatenate([r[0, 0, pl.ds(h, PAGE, stride=heads), :] for h in range(heads)], axis=1).astype(BF16)
         for r in v_refs], axis=0)
    update(jnp.dot(qbd, kt, preferred_element_type=F32), v, None)

    @pl.when(pg == pl.num_programs(1) - 1)
    def _():
        rows = qbd.shape[0]
        tpad = kn_ref.shape[1]
        tq = lax.broadcasted_iota(jnp.int32, (rows, tpad), 0) % t_new
        tk = lax.broadcasted_iota(jnp.int32, (rows, tpad), 1)
        s_new = lax.dot_general(qbd, kn_ref[0].astype(BF16), (((1,), (1,)), ((), ())), preferred_element_type=F32)
        update(s_new, vn_ref[0].astype(BF16), tk <= tq)
        out = acc_ref[...] / l_ref[...]
        lam, gn = lam_ref[...], gn_ref[...]
        for h in range(heads):
            blk = out[h * 2 * t_new:(h + 1) * 2 * t_new, h * HEAD_W:(h + 1) * HEAD_W]
            o_ref[0, :, h * HEAD_W:(h + 1) * HEAD_W] = _diff_norm(blk[:t_new], blk[t_new:], lam, gn, out_scale, 1)


def _sample_attn(page_table, qbd, cache_k, cache_v, k_new, v_new, lam_row, gn_row, *, layer, pages, t_new, out_scale):
    b, rows, w = qbd.shape
    heads = w // HEAD_W
    n_pages = page_table.shape[1]
    tpad = k_new.shape[1]

    def k_spec(i):
        return pl.BlockSpec((1, 1, heads, 2, HEAD_DIM, PAGE), lambda bi, pg, pt: (layer, pt[bi, pg * pages + i], 0, 0, 0, 0))

    def v_spec(i):
        return pl.BlockSpec((1, 1, PAGE * heads, HEAD_W), lambda bi, pg, pt: (layer, pt[bi, pg * pages + i], 0, 0))

    per_b = lambda shape: pl.BlockSpec(shape, lambda bi, pg, pt: (bi, 0, 0))
    const = lambda shape: pl.BlockSpec(shape, lambda bi, pg, pt: (0, 0))
    grid_spec = pltpu.PrefetchScalarGridSpec(
        num_scalar_prefetch=1, grid=(b, n_pages // pages),
        in_specs=[per_b((1, rows, w))] + [k_spec(i) for i in range(pages)] + [v_spec(i) for i in range(pages)]
                 + [per_b((1, tpad, w)), per_b((1, tpad, w)), const((1, HEAD_W)), const((1, HEAD_W))],
        out_specs=per_b((1, t_new, w)),
        scratch_shapes=[pltpu.VMEM((rows, 1), F32), pltpu.VMEM((rows, 1), F32), pltpu.VMEM((rows, w), F32)])
    return pl.pallas_call(
        functools.partial(_sample_attn_kernel, pages=pages, t_new=t_new, heads=heads, out_scale=out_scale),
        grid_spec=grid_spec,
        out_shape=jax.ShapeDtypeStruct((b, t_new, w), F32),
        compiler_params=_cparams("parallel", "arbitrary"), name="sample_attn",
    )(page_table, qbd, *([cache_k] * pages), *([cache_v] * pages), k_new, v_new, lam_row, gn_row)


def _mix_ffn_kernel(x_ref, oa_ref, ob_ref, pe_ref, g_pre_ref, w_gate_ref, w_o_ref, g_postmix_ref, g_preffn_ref,
                    w_ff_in_ref, w_ff_out_ref, g_postffn_ref, w_pe_ref, w_pg_ref, y_ref, *, d_model, d_ff):
    x = x_ref[...]
    h = _rms(x, g_pre_ref[...]).astype(BF16)
    gates = _sigmoid(jnp.dot(h, w_gate_ref[...], preferred_element_type=F32))
    m = gates[:, :d_model] * oa_ref[...].astype(F32) + gates[:, d_model:] * ob_ref[...].astype(F32)
    x = x + _rms(jnp.dot(m.astype(BF16), w_o_ref[...], preferred_element_type=F32), g_postmix_ref[...])
    h = _rms(x, g_preffn_ref[...]).astype(BF16)
    g = jnp.dot(h, w_ff_in_ref[:, :d_ff], preferred_element_type=F32)
    u = jnp.dot(h, w_ff_in_ref[:, d_ff:], preferred_element_type=F32)
    act = (g * _sigmoid(g) * u).astype(BF16)
    x = x + _rms(jnp.dot(act, w_ff_out_ref[...], preferred_element_type=F32), g_postffn_ref[...])
    pg = _sigmoid(jnp.dot(x.astype(BF16), w_pg_ref[...], preferred_element_type=F32))
    y_ref[...] = x + pg * jnp.dot(pe_ref[...].astype(BF16), w_pe_ref[...], preferred_element_type=F32)


def _mix_ffn(x, oa, ob, pemb, g_pre, w_gate, w_o, g_postmix, g_preffn, w_ff_in, w_ff_out, g_postffn, w_pe, w_pg, *, tm):
    n, d = x.shape
    d_ff = w_ff_out.shape[0]
    tok = lambda w: pl.BlockSpec((tm, w), lambda t: (t, 0))
    weights = (g_pre, w_gate, w_o, g_postmix, g_preffn, w_ff_in, w_ff_out, g_postffn, w_pe, w_pg)
    return pl.pallas_call(
        functools.partial(_mix_ffn_kernel, d_model=d, d_ff=d_ff),
        grid=(n // tm,),
        in_specs=[tok(d), tok(d), tok(d), tok(pemb.shape[1])] + [_resident(a.shape) for a in weights],
        out_specs=tok(d), out_shape=jax.ShapeDtypeStruct((n, d), F32),
        compiler_params=_cparams("parallel"), name="mix_ffn",
    )(x, oa, ob, pemb, *weights)


def _rope_tables(pos):
    half = HEAD_DIM // 2
    inv = 1.0 / (ROPE_THETA ** (jnp.arange(half, dtype=F32) / half))
    ang = pos.astype(F32)[:, None] * inv[None, :]
    cos, sin = jnp.cos(ang), jnp.sin(ang)
    return jnp.tile(jnp.concatenate([cos, cos], -1), (1, 2)), jnp.tile(jnp.concatenate([-sin, sin], -1), (1, 2))


def _tile(n, cap):
    t = min(n, cap)
    assert n % t == 0, (n, cap)
    return t


def kernel(x_prompt, x_sample, cache_k, cache_v, state_hgrn, page_table, p_prompt, p_sample, pre_mix_g, w_in, lam_q1, lam_k1, lam_q2, lam_k2, attn_norm_g, hgrn_lb, hgrn_norm_g, w_o, post_mix_g, pre_ffn_g, w_ff_in, w_ff_out, post_ffn_g, w_pe, w_pg):
    depth = w_in.shape[0]
    bp, sp, d = x_prompt.shape
    bs, ts, _ = x_sample.shape
    heads = d // HEAD_W
    n_pages = page_table.shape[1]
    assert d % HEAD_W == 0 and cache_k.shape[2] == PAGE
    lb_all = jnp.cumsum(jax.nn.softmax(hgrn_lb.astype(F32), axis=0), axis=0)
    cos_p, sin_p = _rope_tables(jnp.arange(sp))
    cos_s, sin_s = _rope_tables(jnp.tile(n_pages * PAGE + jnp.arange(ts), bs))
    ck = jnp.transpose(cache_k, (0, 1, 3, 4, 5, 2))
    cv = cache_v.reshape(cache_v.shape[0], cache_v.shape[1], PAGE * heads, HEAD_W)
    ts_pad = -(-ts // 8) * 8
    pad_t = lambda a: jnp.pad(a.reshape(bs, ts, d), ((0, 0), (0, ts_pad - ts), (0, 0)))
    eye = jnp.eye(2 * heads, dtype=BF16)

    yp, ys = x_prompt.reshape(bp * sp, d), x_sample.reshape(bs * ts, d)
    outs = [[] for _ in range(6)]
    for i in range(depth):
        row = lambda a: a[i].reshape(1, -1).astype(F32)
        wi = w_in[i].astype(BF16)
        w_qkv, w_h, w_gate = wi[:, :3 * d], wi[:, 3 * d:7 * d], wi[:, 7 * d:]
        lam_init = 0.8 - 0.6 * math.exp(-0.3 * i)
        lam = (jnp.exp(jnp.sum(lam_q1[i].astype(F32) * lam_k1[i].astype(F32)))
               - jnp.exp(jnp.sum(lam_q2[i].astype(F32) * lam_k2[i].astype(F32))) + lam_init)
        lam_row = jnp.full((1, HEAD_W), lam, F32)
        gn_row = row(attn_norm_g)
        lb = lb_all[i].reshape(1, d)
        mix_w = (row(pre_mix_g), w_gate, w_o[i].astype(BF16), row(post_mix_g), row(pre_ffn_g), w_ff_in[i].astype(BF16),
                 w_ff_out[i].astype(BF16), row(post_ffn_g), w_pe[i].astype(BF16), w_pg[i].astype(BF16))

        q, k, v, kb, vt = _qkv_proj(yp, row(pre_mix_g), w_qkv, cos_p, sin_p, seqs=bp, tm=_tile(sp, 512), attn_copies=True)
        oa = _prompt_attn(q.reshape(bp, sp, d), kb.reshape(bp, sp, d), vt, lam_row.reshape(HEAD_W, 1),
                          gn_row.reshape(HEAD_W, 1), tq=_tile(sp, 512), out_scale=1.0 - lam_init)
        fq, logf, vi, og = _hgrn_proj(yp, row(pre_mix_g), w_h, lb, tm=_tile(bp * sp, 512))
        r3 = lambda a: a.reshape(bp, sp, d)
        ob, s_p = _gla(r3(fq), r3(logf), r3(vi), r3(og), jnp.zeros((bp, d // HG_W, HG_W, HG_W), F32),
                       row(hgrn_norm_g), tt=_tile(sp, 256), chunk=GLA_CHUNK)
        yp = _mix_ffn(yp, oa.reshape(bp * sp, d), ob.reshape(bp * sp, d), p_prompt[i].reshape(bp * sp, -1), *mix_w,
                      tm=_tile(bp * sp, 256))
        k = jnp.transpose(k.reshape(bp, heads, 2, HEAD_DIM, sp), (0, 4, 1, 2, 3))
        outs[0].append(k); outs[1].append(v.reshape(bp, sp, heads, HEAD_W)); outs[2].append(s_p)

        q, k, v = _qkv_proj(ys, row(pre_mix_g), w_qkv, cos_s, sin_s, seqs=1, tm=_tile(bs * ts, 512), attn_copies=False)
        qt = q.reshape(bs, ts, 2 * heads, HEAD_DIM).transpose(0, 2, 1, 3)
        qbd = (qt[:, :, :, None, :] * eye[None, :, None, :, None]).reshape(bs, 2 * heads * ts, d)
        oa = _sample_attn(page_table, qbd, ck, cv, pad_t(k), pad_t(v), lam_row, gn_row, layer=i,
                          pages=math.gcd(n_pages, 8), t_new=ts, out_scale=1.0 - lam_init)
        fq, logf, vi, og = _hgrn_proj(ys, row(pre_mix_g), w_h, lb, tm=_tile(bs * ts, 512))
        ob, s_s = _gla(pad_t(fq), pad_t(logf), pad_t(vi), pad_t(og), state_hgrn[i].astype(F32), row(hgrn_norm_g),
                       tt=ts_pad, chunk=ts_pad)
        ys = _mix_ffn(ys, oa.reshape(bs * ts, d).astype(BF16), ob[:, :ts].reshape(bs * ts, d), p_sample[i].reshape(bs * ts, -1),
                      *mix_w, tm=_tile(bs * ts, 256))
        outs[3].append(k.reshape(bs, ts, heads, 2, HEAD_DIM)); outs[4].append(v.reshape(bs, ts, heads, HEAD_W)); outs[5].append(s_s)

    st = [jnp.stack(o) for o in outs]
    return (yp.reshape(bp, sp, d), ys.reshape(bs, ts, d), st[0], st[1], st[2], st[3], st[4], st[5])
```

```python
import functools
import math

import jax
import jax.numpy as jnp
import numpy as np
from jax import lax
from jax.experimental import pallas as pl
from jax.experimental.pallas import tpu as pltpu

BF16 = jnp.bfloat16
F32 = jnp.float32

LANES = 128
MXU_COLS = 256
HEAD_DIM = 64
HEAD_W = 2 * HEAD_DIM
HG_W = 128
PAGE = 128
ROPE_THETA = 10000.0
EPS = 1e-6
NEG = -1e30
LOG2E = math.log2(math.e)
Q_SCALE = HEAD_DIM ** -0.5 * LOG2E
GLA_CHUNK = 128
ATTN_UNROLL = 4
GLA_MAX_CHUNK_DECAY = 80.0
GLA_SUB = 16
VMEM_LIMIT = 56 * 1024 * 1024


def _cparams(*sem):
    return pltpu.CompilerParams(dimension_semantics=sem, vmem_limit_bytes=VMEM_LIMIT)


def _resident(shape):
    nd = len(shape)
    return pl.BlockSpec(shape, lambda *_: (0,) * nd, pipeline_mode=pl.Buffered(1))


def _rms(x, g):
    return x * lax.rsqrt(jnp.mean(x * x, axis=-1, keepdims=True) + EPS) * g


def _sigmoid(x):
    return 1.0 / (1.0 + jnp.exp(-x))


def _qkv_kernel(x_ref, g_ref, w_ref, cos_ref, sin_ref, q_ref, k_ref, v_ref, *copies, d_model):
    h = _rms(x_ref[...], g_ref[...]).astype(BF16)
    cos, sin = cos_ref[...], sin_ref[...]
    lane = lax.broadcasted_iota(jnp.int32, cos.shape, 1)
    low_half = (lane % HEAD_DIM) < (HEAD_DIM // 2)

    def rope(z):
        zr = jnp.where(low_half, pltpu.roll(z, LANES - HEAD_DIM // 2, 1), pltpu.roll(z, HEAD_DIM // 2, 1))
        return z * cos + zr * sin

    wide = {}
    for c in range(d_model // LANES):
        sl = slice(c * LANES, (c + 1) * LANES)
        blk, sub = divmod(c * LANES, MXU_COLS)
        if sub == 0:
            for part in range(3):
                lo = part * d_model + blk * MXU_COLS
                wide[part] = jnp.dot(h, w_ref[:, lo:lo + MXU_COLS], preferred_element_type=F32)
        q, k, v = (wide[part][:, sub:sub + LANES] for part in range(3))
        q_ref[:, sl] = (rope(q) * Q_SCALE).astype(BF16)
        k = rope(k)
        v_ref[:, sl] = v
        if copies:
            kb_ref, vt_ref = copies
            k_ref[0, 2 * c:2 * c + 2] = k.T.reshape(2, HEAD_DIM, k.shape[0])
            kb_ref[:, sl] = k.astype(BF16)
            vt_ref[0, c] = v.T.astype(BF16)
        else:
            k_ref[:, sl] = k


def _qkv_proj(x, g, w_qkv, cos, sin, *, seqs, tm, attn_copies):
    n, d = x.shape
    per_seq = n // seqs
    nt = per_seq // tm
    tok = pl.BlockSpec((tm, d), lambda b, t: (b * nt + t, 0))
    tab = pl.BlockSpec((tm, LANES), lambda b, t: (t, 0))
    out_shape = [jax.ShapeDtypeStruct((n, d), BF16), jax.ShapeDtypeStruct((n, d), F32), jax.ShapeDtypeStruct((n, d), F32)]
    out_specs = [tok, tok, tok]
    if attn_copies:
        maps = d // HEAD_DIM
        out_shape[1] = jax.ShapeDtypeStruct((seqs, maps, HEAD_DIM, per_seq), F32)
        out_specs[1] = pl.BlockSpec((1, maps, HEAD_DIM, tm), lambda b, t: (b, 0, 0, t))
        out_shape += [jax.ShapeDtypeStruct((n, d), BF16), jax.ShapeDtypeStruct((seqs, d // HEAD_W, HEAD_W, per_seq), BF16)]
        out_specs += [tok, pl.BlockSpec((1, d // HEAD_W, HEAD_W, tm), lambda b, t: (b, 0, 0, t))]
    return pl.pallas_call(
        functools.partial(_qkv_kernel, d_model=d),
        grid=(seqs, nt),
        in_specs=[tok, _resident((1, d)), _resident(w_qkv.shape), tab, tab],
        out_specs=out_specs, out_shape=out_shape,
        compiler_params=_cparams("parallel", "parallel"), name="qkv_proj",
    )(x, g, w_qkv, cos, sin)


def _hgrn_proj_kernel(x_ref, g_ref, w_ref, lb_ref, fq_ref, logf_ref, vi_ref, og_ref, *, d_model):
    h = _rms(x_ref[...], g_ref[...]).astype(BF16)
    for c in range(d_model // MXU_COLS):
        sl = slice(c * MXU_COLS, (c + 1) * MXU_COLS)

        def proj(part):
            lo = part * d_model + c * MXU_COLS
            return jnp.dot(h, w_ref[:, lo:lo + MXU_COLS], preferred_element_type=F32)

        hq, hf, hi, hg = proj(0), proj(1), proj(2), proj(3)
        lb = lb_ref[:, sl]
        fq_ref[:, sl] = (hq * _sigmoid(hq)).astype(BF16)
        logf_ref[:, sl] = jnp.log(lb + (1.0 - lb) * _sigmoid(hf))
        vi_ref[:, sl] = hi.astype(BF16)
        og_ref[:, sl] = (hg * _sigmoid(hg)).astype(BF16)


def _hgrn_proj(x, g, w_h, lb, *, tm):
    n, d = x.shape
    tok = pl.BlockSpec((tm, d), lambda t: (t, 0))
    return pl.pallas_call(
        functools.partial(_hgrn_proj_kernel, d_model=d),
        grid=(n // tm,),
        in_specs=[tok, _resident((1, d)), _resident(w_h.shape), _resident((1, d))],
        out_specs=[tok, tok, tok, tok],
        out_shape=[jax.ShapeDtypeStruct((n, d), BF16), jax.ShapeDtypeStruct((n, d), F32),
                   jax.ShapeDtypeStruct((n, d), BF16), jax.ShapeDtypeStruct((n, d), BF16)],
        compiler_params=_cparams("parallel"), name="hgrn_proj",
    )(x, g, w_h, lb)


def _chunk_cumsum(g, chunk):
    row = lax.broadcasted_iota(jnp.int32, g.shape, 0) % chunk
    cum, sh = g, 1
    while sh < chunk:
        cum = cum + jnp.where(row >= sh, pltpu.roll(cum, sh, 0), 0.0)
        sh *= 2
    return cum


def _gla_exact_chunk(base, chunk, heads, fq_ref, logf_ref, vi_ref, og_ref, ob_ref, st_ref, gn):
    sub = min(GLA_SUB, chunk)
    trow = lax.broadcasted_iota(jnp.int32, (sub, 1), 0)

    def sub_body(u, carry):
        rows = pl.ds(pl.multiple_of(base + u * sub, sub), sub)
        g = logf_ref[0, rows, :]
        cum = _chunk_cumsum(g, sub)
        total = cum[sub - 1:sub, :]
        fk = 1.0 - jnp.exp(g)
        fq = fq_ref[0, rows, :].astype(F32)
        v = vi_ref[0, rows, :]
        v32 = v.astype(F32)
        q_in = (fq * jnp.exp(cum)).astype(BF16)
        k_end = (fk * jnp.exp(total - cum)).astype(BF16)
        dec = jnp.exp(total)
        for h in range(heads):
            sl = slice(h * HG_W, (h + 1) * HG_W)
            st = st_ref[h]
            o = lax.dot_general(q_in[:, sl], st.astype(BF16), (((1,), (1,)), ((), ())), preferred_element_type=F32)

            def pair_col(s, o):
                cum_s, fk_s, v_s = (jnp.sum(jnp.where(trow == s, x[:, sl], 0.0), axis=0, keepdims=True)
                                    for x in (cum, fk, v32))
                w = fq[:, sl] * jnp.exp(jnp.minimum(cum[:, sl] - cum_s, 0.0)) * fk_s
                a = jnp.where(trow >= s, jnp.sum(w, axis=1, keepdims=True), 0.0)
                return o + a * v_s

            o = lax.fori_loop(0, sub, pair_col, o)
            st_ref[h] = st * dec[:, sl] + lax.dot_general(v[:, sl], k_end[:, sl], (((0,), (0,)), ((), ())),
                                                          preferred_element_type=F32)
            o = _rms(o, gn) * og_ref[0, rows, sl].astype(F32)
            ob_ref[0, rows, sl] = o.astype(BF16)
        return carry

    lax.fori_loop(0, chunk // sub, sub_body, 0)


def _gla_kernel(fq_ref, logf_ref, vi_ref, og_ref, s0_ref, gn_ref, ob_ref, s_out_ref, st_ref, *, chunk, heads):
    t = pl.program_id(1)

    @pl.when(t == 0)
    def _():
        for h in range(heads):
            st_ref[h] = s0_ref[0, h].T

    tt = fq_ref.shape[1]
    tri = lax.broadcasted_iota(jnp.int32, (chunk, chunk), 0) >= lax.broadcasted_iota(jnp.int32, (chunk, chunk), 1)
    gn = gn_ref[...]

    def factored_chunk(c, carry):
        rows = pl.ds(pl.multiple_of(c * chunk, chunk), chunk)
        g = logf_ref[0, rows, :]
        cum = _chunk_cumsum(g, chunk)
        total = cum[chunk - 1:chunk, :]
        fk = 1.0 - jnp.exp(g)
        q_in = (fq_ref[0, rows, :].astype(F32) * jnp.exp(cum)).astype(BF16)
        k_out = (fk * jnp.exp(-cum)).astype(BF16)
        k_end = (fk * jnp.exp(total - cum)).astype(BF16)
        dec = jnp.exp(total)
        v = vi_ref[0, rows, :]
        for h in range(heads):
            sl = slice(h * HG_W, (h + 1) * HG_W)
            st = st_ref[h]
            o = lax.dot_general(q_in[:, sl], st.astype(BF16), (((1,), (1,)), ((), ())), preferred_element_type=F32)
            a = lax.dot_general(q_in[:, sl], k_out[:, sl], (((1,), (1,)), ((), ())), preferred_element_type=F32)
            a = jnp.where(tri, a, 0.0).astype(BF16)
            o = o + jnp.dot(a, v[:, sl], preferred_element_type=F32)
            st_ref[h] = st * dec[:, sl] + lax.dot_general(v[:, sl], k_end[:, sl], (((0,), (0,)), ((), ())),
                                                          preferred_element_type=F32)
            o = _rms(o, gn) * og_ref[0, rows, sl].astype(F32)
            ob_ref[0, rows, sl] = o.astype(BF16)
        return carry

    def exact_chunk(c, carry):
        _gla_exact_chunk(c * chunk, chunk, heads, fq_ref, logf_ref, vi_ref, og_ref, ob_ref, st_ref, gn)
        return carry

    chunk_totals = jnp.sum(logf_ref[0].reshape(tt // chunk, chunk, -1), axis=1)
    factorable = jnp.min(chunk_totals) >= -GLA_MAX_CHUNK_DECAY

    @pl.when(factorable)
    def _():
        lax.fori_loop(0, tt // chunk, factored_chunk, 0)

    @pl.when(jnp.logical_not(factorable))
    def _():
        lax.fori_loop(0, tt // chunk, exact_chunk, 0)

    @pl.when(t == pl.num_programs(1) - 1)
    def _():
        for h in range(heads):
            s_out_ref[0, h] = st_ref[h].T


def _gla(fq, logf, vi, og, s0, gn, *, tt, chunk):
    b, t, w = fq.shape
    heads = w // HG_W
    tok = pl.BlockSpec((1, tt, w), lambda i, j: (i, j, 0))
    st = pl.BlockSpec((1, heads, HG_W, HG_W), lambda i, j: (i, 0, 0, 0))
    return pl.pallas_call(
        functools.partial(_gla_kernel, chunk=chunk, heads=heads),
        grid=(b, t // tt),
        in_specs=[tok, tok, tok, tok, st, _resident((1, HG_W))],
        out_specs=[tok, st],
        out_shape=[jax.ShapeDtypeStruct((b, t, w), BF16), jax.ShapeDtypeStruct(s0.shape, F32)],
        scratch_shapes=[pltpu.VMEM((heads, HG_W, HG_W), F32)],
        compiler_params=_cparams("parallel", "arbitrary"), name="gla",
    )(fq, logf, vi, og, s0, gn)


def _diff_norm(o0, o1, lam, gn, out_scale, axis):
    o = o0 - lam * o1
    return o * lax.rsqrt(jnp.mean(o * o, axis=axis, keepdims=True) + EPS) * gn * out_scale


def _prompt_attn_kernel(q_ref, k_ref, vt_ref, lam_ref, gn_ref, o_ref, m_ref, l_ref, acc_ref, s0_ref, s1_ref, *, out_scale):
    qi = pl.program_id(2)
    tq = q_ref.shape[1]
    tk = tq
    q = q_ref[0]
    lane = lax.broadcasted_iota(jnp.int32, q.shape, 1)
    zero = jnp.zeros_like(q)
    qm = (jnp.where(lane < HEAD_DIM, q, zero), jnp.where(lane >= HEAD_DIM, q, zero))
    s_refs = (s0_ref, s1_ref)
    m_ref[...] = jnp.full(m_ref.shape, NEG, F32)
    l_ref[...] = jnp.zeros(l_ref.shape, F32)
    acc_ref[...] = jnp.zeros(acc_ref.shape, F32)

    def rows(j):
        return pl.ds(pl.multiple_of(j * tk, tk), tk)

    def scores(j, mp):
        s_refs[mp][...] = lax.dot_general(k_ref[0, rows(j), :], qm[mp], (((1,), (1,)), ((), ())),
                                          preferred_element_type=F32)

    def consume(j, mp, mask):
        s = s_refs[mp][...]
        if mask is not None:
            s = jnp.where(mask, s, NEG)
        m_old = m_ref[mp]
        m_new = jnp.maximum(m_old, jnp.max(s, axis=0, keepdims=True))
        p = jnp.exp2(s - m_new)
        if mask is not None:
            p = jnp.where(mask, p, 0.0)
        alpha = jnp.exp2(m_old - m_new)
        l_ref[mp] = l_ref[mp] * alpha + jnp.sum(p, axis=0, keepdims=True)
        acc_ref[mp] = acc_ref[mp] * alpha + jnp.dot(vt_ref[0, 0, :, rows(j)], p.astype(BF16), preferred_element_type=F32)
        m_ref[mp] = m_new

    def full_tile(j):
        scores(j, 1)
        consume(j, 0, None)
        scores(j + 1, 0)
        consume(j, 1, None)

    def single(j, carry):
        full_tile(j)
        return carry

    def group(i, carry):
        for u in range(ATTN_UNROLL):
            full_tile(rem + ATTN_UNROLL * i + u)
        return carry

    rem = qi % ATTN_UNROLL
    scores(0, 0)
    lax.fori_loop(0, rem, single, 0)
    lax.fori_loop(0, qi // ATTN_UNROLL, group, 0)
    scores(qi, 1)
    causal = lax.broadcasted_iota(jnp.int32, (tk, tq), 0) <= lax.broadcasted_iota(jnp.int32, (tk, tq), 1)
    consume(qi, 0, causal)
    consume(qi, 1, causal)

    o = _diff_norm(acc_ref[0] / l_ref[0], acc_ref[1] / l_ref[1], lam_ref[...], gn_ref[...], out_scale, 0)
    o_ref[0] = o.T.astype(BF16)


def _prompt_attn(q, kb, vt, lam_col, gn_col, *, tq, out_scale):
    b, s, w = q.shape
    heads = w // HEAD_W
    return pl.pallas_call(
        functools.partial(_prompt_attn_kernel, out_scale=out_scale),
        grid=(b, heads, s // tq),
        in_specs=[pl.BlockSpec((1, tq, HEAD_W), lambda i, h, j: (i, j, h)),
                  pl.BlockSpec((1, s, HEAD_W), lambda i, h, j: (i, 0, h)),
                  pl.BlockSpec((1, 1, HEAD_W, s), lambda i, h, j: (i, h, 0, 0)),
                  _resident((HEAD_W, 1)), _resident((HEAD_W, 1))],
        out_specs=pl.BlockSpec((1, tq, HEAD_W), lambda i, h, j: (i, j, h)),
        out_shape=jax.ShapeDtypeStruct((b, s, w), BF16),
        scratch_shapes=[pltpu.VMEM((2, 1, tq), F32), pltpu.VMEM((2, 1, tq), F32), pltpu.VMEM((2, HEAD_W, tq), F32),
                        pltpu.VMEM((tq, tq), F32), pltpu.VMEM((tq, tq), F32)],
        compiler_params=_cparams("parallel", "parallel", "arbitrary"), name="prompt_attn",
    )(q, kb, vt, lam_col, gn_col)


def _sample_attn_kernel(pt_ref, qbd_ref, *refs, pages, t_new, heads, out_scale):
    k_refs, v_refs = refs[:pages], refs[pages:2 * pages]
    kn_ref, vn_ref, lam_ref, gn_ref, o_ref, m_ref, l_ref, acc_ref = refs[2 * pages:]
    pg = pl.program_id(1)

    @pl.when(pg == 0)
    def _():
        m_ref[...] = jnp.full(m_ref.shape, NEG, F32)
        l_ref[...] = jnp.zeros(l_ref.shape, F32)
        acc_ref[...] = jnp.zeros(acc_ref.shape, F32)

    qbd = qbd_ref[0]

    def update(s, v, mask):
        if mask is not None:
            s = jnp.where(mask, s, NEG)
        m_old = m_ref[...]
        m_new = jnp.maximum(m_old, jnp.max(s, axis=1, keepdims=True))
        p = jnp.exp2(s - m_new)
        if mask is not None:
            p = jnp.where(mask, p, 0.0)
        alpha = jnp.exp2(m_old - m_new)
        l_ref[...] = l_ref[...] * alpha + jnp.sum(p, axis=1, keepdims=True)
        acc_ref[...] = acc_ref[...] * alpha + jnp.dot(p.astype(BF16), v, preferred_element_type=F32)
        m_ref[...] = m_new

    w = qbd.shape[1]
    kt = jnp.concatenate([r[0, 0].reshape(w, PAGE).astype(BF16) for r in k_refs], axis=1)
    v = jnp.concatenate(
        [jnp.concatenate([r[0, 0, pl.ds(h, PAGE, stride=heads), :] for h in range(heads)], axis=1).astype(BF16)
         for r in v_refs], axis=0)
    update(jnp.dot(qbd, kt, preferred_element_type=F32), v, None)

    @pl.when(pg == pl.num_programs(1) - 1)
    def _():
        rows = qbd.shape[0]
        tpad = kn_ref.shape[1]
        tq = lax.broadcasted_iota(jnp.int32, (rows, tpad), 0) % t_new
        tk = lax.broadcasted_iota(jnp.int32, (rows, tpad), 1)
        s_new = lax.dot_general(qbd, kn_ref[0].astype(BF16), (((1,), (1,)), ((), ())), preferred_element_type=F32)
        update(s_new, vn_ref[0].astype(BF16), tk <= tq)
        out = acc_ref[...] / l_ref[...]
        lam, gn = lam_ref[...], gn_ref[...]
        for h in range(heads):
            blk = out[h * 2 * t_new:(h + 1) * 2 * t_new, h * HEAD_W:(h + 1) * HEAD_W]
            o_ref[0, :, h * HEAD_W:(h + 1) * HEAD_W] = _diff_norm(blk[:t_new], blk[t_new:], lam, gn, out_scale, 1)


def _sample_attn(page_table, qbd, cache_k, cache_v, k_new, v_new, lam_row, gn_row, *, layer, pages, t_new, out_scale):
    b, rows, w = qbd.shape
    heads = w // HEAD_W
    n_pages = page_table.shape[1]
    tpad = k_new.shape[1]

    def k_spec(i):
        return pl.BlockSpec((1, 1, heads, 2, HEAD_DIM, PAGE), lambda bi, pg, pt: (layer, pt[bi, pg * pages + i], 0, 0, 0, 0))

    def v_spec(i):
        return pl.BlockSpec((1, 1, PAGE * heads, HEAD_W), lambda bi, pg, pt: (layer, pt[bi, pg * pages + i], 0, 0))

    per_b = lambda shape: pl.BlockSpec(shape, lambda bi, pg, pt: (bi, 0, 0))
    const = lambda shape: pl.BlockSpec(shape, lambda bi, pg, pt: (0, 0))
    grid_spec = pltpu.PrefetchScalarGridSpec(
        num_scalar_prefetch=1, grid=(b, n_pages // pages),
        in_specs=[per_b((1, rows, w))] + [k_spec(i) for i in range(pages)] + [v_spec(i) for i in range(pages)]
                 + [per_b((1, tpad, w)), per_b((1, tpad, w)), const((1, HEAD_W)), const((1, HEAD_W))],
        out_specs=per_b((1, t_new, w)),
        scratch_shapes=[pltpu.VMEM((rows, 1), F32), pltpu.VMEM((rows, 1), F32), pltpu.VMEM((rows, w), F32)])
    return pl.pallas_call(
        functools.partial(_sample_attn_kernel, pages=pages, t_new=t_new, heads=heads, out_scale=out_scale),
        grid_spec=grid_spec,
        out_shape=jax.ShapeDtypeStruct((b, t_new, w), F32),
        compiler_params=_cparams("parallel", "arbitrary"), name="sample_attn",
    )(page_table, qbd, *([cache_k] * pages), *([cache_v] * pages), k_new, v_new, lam_row, gn_row)


def _mix_ffn_kernel(x_ref, oa_ref, ob_ref, pe_ref, g_pre_ref, w_gate_ref, w_o_ref, g_postmix_ref, g_preffn_ref,
                    w_ff_in_ref, w_ff_out_ref, g_postffn_ref, w_pe_ref, w_pg_ref, y_ref, *, d_model, d_ff):
    x = x_ref[...]
    h = _rms(x, g_pre_ref[...]).astype(BF16)
    gates = _sigmoid(jnp.dot(h, w_gate_ref[...], preferred_element_type=F32))
    m = gates[:, :d_model] * oa_ref[...].astype(F32) + gates[:, d_model:] * ob_ref[...].astype(F32)
    x = x + _rms(jnp.dot(m.astype(BF16), w_o_ref[...], preferred_element_type=F32), g_postmix_ref[...])
    h = _rms(x, g_preffn_ref[...]).astype(BF16)
    g = jnp.dot(h, w_ff_in_ref[:, :d_ff], preferred_element_type=F32)
    u = jnp.dot(h, w_ff_in_ref[:, d_ff:], preferred_element_type=F32)
    act = (g * _sigmoid(g) * u).astype(BF16)
    x = x + _rms(jnp.dot(act, w_ff_out_ref[...], preferred_element_type=F32), g_postffn_ref[...])
    pg = _sigmoid(jnp.dot(x.astype(BF16), w_pg_ref[...], preferred_element_type=F32))
    y_ref[...] = x + pg * jnp.dot(pe_ref[...].astype(BF16), w_pe_ref[...], preferred_element_type=F32)


def _mix_ffn(x, oa, ob, pemb, g_pre, w_gate, w_o, g_postmix, g_preffn, w_ff_in, w_ff_out, g_postffn, w_pe, w_pg, *, tm):
    n, d = x.shape
    d_ff = w_ff_out.shape[0]
    tok = lambda w: pl.BlockSpec((tm, w), lambda t: (t, 0))
    weights = (g_pre, w_gate, w_o, g_postmix, g_preffn, w_ff_in, w_ff_out, g_postffn, w_pe, w_pg)
    return pl.pallas_call(
        functools.partial(_mix_ffn_kernel, d_model=d, d_ff=d_ff),
        grid=(n // tm,),
        in_specs=[tok(d), tok(d), tok(d), tok(pemb.shape[1])] + [_resident(a.shape) for a in weights],
        out_specs=tok(d), out_shape=jax.ShapeDtypeStruct((n, d), F32),
        compiler_params=_cparams("parallel"), name="mix_ffn",
    )(x, oa, ob, pemb, *weights)


def _rope_tables(pos):
    half = HEAD_DIM // 2
    inv = 1.0 / (ROPE_THETA ** (jnp.arange(half, dtype=F32) / half))
    ang = pos.astype(F32)[:, None] * inv[None, :]
    cos, sin = jnp.cos(ang), jnp.sin(ang)
    return jnp.tile(jnp.concatenate([cos, cos], -1), (1, 2)), jnp.tile(jnp.concatenate([-sin, sin], -1), (1, 2))


def _tile(n, cap):
    t = min(n, cap)
    assert n % t == 0, (n, cap)
    return t


def kernel(x_prompt, x_sample, cache_k, cache_v, state_hgrn, page_table, p_prompt, p_sample, pre_mix_g, w_in, lam_q1, lam_k1, lam_q2, lam_k2, attn_norm_g, hgrn_lb, hgrn_norm_g, w_o, post_mix_g, pre_ffn_g, w_ff_in, w_ff_out, post_ffn_g, w_pe, w_pg):
    depth = w_in.shape[0]
    bp, sp, d = x_prompt.shape
    bs, ts, _ = x_sample.shape
    heads = d // HEAD_W
    n_pages = page_table.shape[1]
    assert d % HEAD_W == 0 and cache_k.shape[2] == PAGE
    lb_all = jnp.cumsum(jax.nn.softmax(hgrn_lb.astype(F32), axis=0), axis=0)
    cos_p, sin_p = _rope_tables(jnp.arange(sp))
    cos_s, sin_s = _rope_tables(jnp.tile(n_pages * PAGE + jnp.arange(ts), bs))
    ck = jnp.transpose(cache_k, (0, 1, 3, 4, 5, 2))
    cv = cache_v.reshape(cache_v.shape[0], cache_v.shape[1], PAGE * heads, HEAD_W)
    ts_pad = -(-ts // 8) * 8
    pad_t = lambda a: jnp.pad(a.reshape(bs, ts, d), ((0, 0), (0, ts_pad - ts), (0, 0)))
    eye = jnp.eye(2 * heads, dtype=BF16)

    yp, ys = x_prompt.reshape(bp * sp, d), x_sample.reshape(bs * ts, d)
    outs = [[] for _ in range(6)]
    for i in range(depth):
        row = lambda a: a[i].reshape(1, -1).astype(F32)
        wi = w_in[i].astype(BF16)
        w_qkv, w_h, w_gate = wi[:, :3 * d], wi[:, 3 * d:7 * d], wi[:, 7 * d:]
        lam_init = 0.8 - 0.6 * math.exp(-0.3 * i)
        lam = (jnp.exp(jnp.sum(lam_q1[i].astype(F32) * lam_k1[i].astype(F32)))
               - jnp.exp(jnp.sum(lam_q2[i].astype(F32) * lam_k2[i].astype(F32))) + lam_init)
        lam_row = jnp.full((1, HEAD_W), lam, F32)
        gn_row = row(attn_norm_g)
        lb = lb_all[i].reshape(1, d)
        mix_w = (row(pre_mix_g), w_gate, w_o[i].astype(BF16), row(post_mix_g), row(pre_ffn_g), w_ff_in[i].astype(BF16),
                 w_ff_out[i].astype(BF16), row(post_ffn_g), w_pe[i].astype(BF16), w_pg[i].astype(BF16))

        q, k, v, kb, vt = _qkv_proj(yp, row(pre_mix_g), w_qkv, cos_p, sin_p, seqs=bp, tm=_tile(sp, 512), attn_copies=True)
        oa = _prompt_attn(q.reshape(bp, sp, d), kb.reshape(bp, sp, d), vt, lam_row.reshape(HEAD_W, 1),
                          gn_row.reshape(HEAD_W, 1), tq=_tile(sp, 512), out_scale=1.0 - lam_init)
        fq, logf, vi, og = _hgrn_proj(yp, row(pre_mix_g), w_h, lb, tm=_tile(bp * sp, 512))
        r3 = lambda a: a.reshape(bp, sp, d)
        ob, s_p = _gla(r3(fq), r3(logf), r3(vi), r3(og), jnp.zeros((bp, d // HG_W, HG_W, HG_W), F32),
                       row(hgrn_norm_g), tt=_tile(sp, 256), chunk=GLA_CHUNK)
        yp = _mix_ffn(yp, oa.reshape(bp * sp, d), ob.reshape(bp * sp, d), p_prompt[i].reshape(bp * sp, -1), *mix_w,
                      tm=_tile(bp * sp, 512))
        k = jnp.transpose(k.reshape(bp, heads, 2, HEAD_DIM, sp), (0, 4, 1, 2, 3))
        outs[0].append(k); outs[1].append(v.reshape(bp, sp, heads, HEAD_W)); outs[2].append(s_p)

        q, k, v = _qkv_proj(ys, row(pre_mix_g), w_qkv, cos_s, sin_s, seqs=1, tm=_tile(bs * ts, 512), attn_copies=False)
        qt = q.reshape(bs, ts, 2 * heads, HEAD_DIM).transpose(0, 2, 1, 3)
        qbd = (qt[:, :, :, None, :] * eye[None, :, None, :, None]).reshape(bs, 2 * heads * ts, d)
        oa = _sample_attn(page_table, qbd, ck, cv, pad_t(k), pad_t(v), lam_row, gn_row, layer=i,
                          pages=math.gcd(n_pages, 8), t_new=ts, out_scale=1.0 - lam_init)
        fq, logf, vi, og = _hgrn_proj(ys, row(pre_mix_g), w_h, lb, tm=_tile(bs * ts, 512))
        ob, s_s = _gla(pad_t(fq), pad_t(logf), pad_t(vi), pad_t(og), state_hgrn[i].astype(F32), row(hgrn_norm_g),
                       tt=ts_pad, chunk=ts_pad)
        ys = _mix_ffn(ys, oa.reshape(bs * ts, d).astype(BF16), ob[:, :ts].reshape(bs * ts, d), p_sample[i].reshape(bs * ts, -1),
                      *mix_w, tm=_tile(bs * ts, 256))
        outs[3].append(k.reshape(bs, ts, heads, 2, HEAD_DIM)); outs[4].append(v.reshape(bs, ts, heads, HEAD_W)); outs[5].append(s_s)

    st = [jnp.stack(o) for o in outs]
    return (yp.reshape(bp, sp, d), ys.reshape(bs, ts, d), st[0], st[1], st[2], st[3], st[4], st[5])
```

```python
import functools
import math

import jax
import jax.numpy as jnp
import numpy as np
from jax import lax
from jax.experimental import pallas as pl
from jax.experimental.pallas import tpu as pltpu

BF16 = jnp.bfloat16
F32 = jnp.float32

LANES = 128
MXU_COLS = 256
HEAD_DIM = 64
HEAD_W = 2 * HEAD_DIM
HG_W = 128
PAGE = 128
ROPE_THETA = 10000.0
EPS = 1e-6
NEG = -1e30
LOG2E = math.log2(math.e)
Q_SCALE = HEAD_DIM ** -0.5 * LOG2E
GLA_CHUNK = 128
ATTN_UNROLL = 4
GLA_MAX_CHUNK_DECAY = 80.0
GLA_SUB = 16
VMEM_LIMIT = 56 * 1024 * 1024


def _cparams(*sem):
    return pltpu.CompilerParams(dimension_semantics=sem, vmem_limit_bytes=VMEM_LIMIT)


def _resident(shape):
    nd = len(shape)
    return pl.BlockSpec(shape, lambda *_: (0,) * nd, pipeline_mode=pl.Buffered(1))


def _rms(x, g):
    return x * lax.rsqrt(jnp.mean(x * x, axis=-1, keepdims=True) + EPS) * g


def _sigmoid(x):
    return 1.0 / (1.0 + jnp.exp(-x))


def _qkv_kernel(x_ref, g_ref, w_ref, cos_ref, sin_ref, q_ref, k_ref, v_ref, *copies, d_model):
    h = _rms(x_ref[...], g_ref[...]).astype(BF16)
    cos, sin = cos_ref[...], sin_ref[...]
    lane = lax.broadcasted_iota(jnp.int32, cos.shape, 1)
    low_half = (lane % HEAD_DIM) < (HEAD_DIM // 2)

    def rope(z):
        zr = jnp.where(low_half, pltpu.roll(z, LANES - HEAD_DIM // 2, 1), pltpu.roll(z, HEAD_DIM // 2, 1))
        return z * cos + zr * sin

    wide = {}
    for c in range(d_model // LANES):
        sl = slice(c * LANES, (c + 1) * LANES)
        blk, sub = divmod(c * LANES, MXU_COLS)
        if sub == 0:
            for part in range(3):
                lo = part * d_model + blk * MXU_COLS
                wide[part] = jnp.dot(h, w_ref[:, lo:lo + MXU_COLS], preferred_element_type=F32)
        q, k, v = (wide[part][:, sub:sub + LANES] for part in range(3))
        q_ref[:, sl] = (rope(q) * Q_SCALE).astype(BF16)
        k = rope(k)
        v_ref[:, sl] = v
        if copies:
            kb_ref, vt_ref = copies
            k_ref[0, 2 * c:2 * c + 2] = k.T.reshape(2, HEAD_DIM, k.shape[0])
            kb_ref[:, sl] = k.astype(BF16)
            vt_ref[0, c] = v.T.astype(BF16)
        else:
            k_ref[:, sl] = k


def _qkv_proj(x, g, w_qkv, cos, sin, *, seqs, tm, attn_copies):
    n, d = x.shape
    per_seq = n // seqs
    nt = per_seq // tm
    tok = pl.BlockSpec((tm, d), lambda b, t: (b * nt + t, 0))
    tab = pl.BlockSpec((tm, LANES), lambda b, t: (t, 0))
    out_shape = [jax.ShapeDtypeStruct((n, d), BF16), jax.ShapeDtypeStruct((n, d), F32), jax.ShapeDtypeStruct((n, d), F32)]
    out_specs = [tok, tok, tok]
    if attn_copies:
        maps = d // HEAD_DIM
        out_shape[1] = jax.ShapeDtypeStruct((seqs, maps, HEAD_DIM, per_seq), F32)
        out_specs[1] = pl.BlockSpec((1, maps, HEAD_DIM, tm), lambda b, t: (b, 0, 0, t))
        out_shape += [jax.ShapeDtypeStruct((n, d), BF16), jax.ShapeDtypeStruct((seqs, d // HEAD_W, HEAD_W, per_seq), BF16)]
        out_specs += [tok, pl.BlockSpec((1, d // HEAD_W, HEAD_W, tm), lambda b, t: (b, 0, 0, t))]
    return pl.pallas_call(
        functools.partial(_qkv_kernel, d_model=d),
        grid=(seqs, nt),
        in_specs=[tok, _resident((1, d)), _resident(w_qkv.shape), tab, tab],
        out_specs=out_specs, out_shape=out_shape,
        compiler_params=_cparams("parallel", "parallel"), name="qkv_proj",
    )(x, g, w_qkv, cos, sin)


def _hgrn_proj_kernel(x_ref, g_ref, w_ref, lb_ref, fq_ref, logf_ref, vi_ref, og_ref, *, d_model):
    h = _rms(x_ref[...], g_ref[...]).astype(BF16)
    for c in range(d_model // MXU_COLS):
        sl = slice(c * MXU_COLS, (c + 1) * MXU_COLS)

        def proj(part):
            lo = part * d_model + c * MXU_COLS
            return jnp.dot(h, w_ref[:, lo:lo + MXU_COLS], preferred_element_type=F32)

        hq, hf, hi, hg = proj(0), proj(1), proj(2), proj(3)
        lb = lb_ref[:, sl]
        fq_ref[:, sl] = (hq * _sigmoid(hq)).astype(BF16)
        logf_ref[:, sl] = jnp.log(lb + (1.0 - lb) * _sigmoid(hf))
        vi_ref[:, sl] = hi.astype(BF16)
        og_ref[:, sl] = (hg * _sigmoid(hg)).astype(BF16)


def _hgrn_proj(x, g, w_h, lb, *, tm):
    n, d = x.shape
    tok = pl.BlockSpec((tm, d), lambda t: (t, 0))
    return pl.pallas_call(
        functools.partial(_hgrn_proj_kernel, d_model=d),
        grid=(n // tm,),
        in_specs=[tok, _resident((1, d)), _resident(w_h.shape), _resident((1, d))],
        out_specs=[tok, tok, tok, tok],
        out_shape=[jax.ShapeDtypeStruct((n, d), BF16), jax.ShapeDtypeStruct((n, d), F32),
                   jax.ShapeDtypeStruct((n, d), BF16), jax.ShapeDtypeStruct((n, d), BF16)],
        compiler_params=_cparams("parallel"), name="hgrn_proj",
    )(x, g, w_h, lb)


def _chunk_cumsum(g, chunk):
    row = lax.broadcasted_iota(jnp.int32, g.shape, 0) % chunk
    cum, sh = g, 1
    while sh < chunk:
        cum = cum + jnp.where(row >= sh, pltpu.roll(cum, sh, 0), 0.0)
        sh *= 2
    return cum


def _gla_exact_chunk(base, chunk, heads, fq_ref, logf_ref, vi_ref, og_ref, ob_ref, st_ref, gn):
    sub = min(GLA_SUB, chunk)
    trow = lax.broadcasted_iota(jnp.int32, (sub, 1), 0)

    def sub_body(u, carry):
        rows = pl.ds(pl.multiple_of(base + u * sub, sub), sub)
        g = logf_ref[0, rows, :]
        cum = _chunk_cumsum(g, sub)
        total = cum[sub - 1:sub, :]
        fk = 1.0 - jnp.exp(g)
        fq = fq_ref[0, rows, :].astype(F32)
        v = vi_ref[0, rows, :]
        v32 = v.astype(F32)
        q_in = (fq * jnp.exp(cum)).astype(BF16)
        k_end = (fk * jnp.exp(total - cum)).astype(BF16)
        dec = jnp.exp(total)
        for h in range(heads):
            sl = slice(h * HG_W, (h + 1) * HG_W)
            st = st_ref[h]
            o = lax.dot_general(q_in[:, sl], st.astype(BF16), (((1,), (1,)), ((), ())), preferred_element_type=F32)

            def pair_col(s, o):
                cum_s, fk_s, v_s = (jnp.sum(jnp.where(trow == s, x[:, sl], 0.0), axis=0, keepdims=True)
                                    for x in (cum, fk, v32))
                w = fq[:, sl] * jnp.exp(jnp.minimum(cum[:, sl] - cum_s, 0.0)) * fk_s
                a = jnp.where(trow >= s, jnp.sum(w, axis=1, keepdims=True), 0.0)
                return o + a * v_s

            o = lax.fori_loop(0, sub, pair_col, o)
            st_ref[h] = st * dec[:, sl] + lax.dot_general(v[:, sl], k_end[:, sl], (((0,), (0,)), ((), ())),
                                                          preferred_element_type=F32)
            o = _rms(o, gn) * og_ref[0, rows, sl].astype(F32)
            ob_ref[0, rows, sl] = o.astype(BF16)
        return carry

    lax.fori_loop(0, chunk // sub, sub_body, 0)


def _gla_kernel(fq_ref, logf_ref, vi_ref, og_ref, s0_ref, gn_ref, ob_ref, s_out_ref, st_ref, *, chunk, heads):
    t = pl.program_id(1)

    @pl.when(t == 0)
    def _():
        for h in range(heads):
            st_ref[h] = s0_ref[0, h].T

    tt = fq_ref.shape[1]
    tri = lax.broadcasted_iota(jnp.int32, (chunk, chunk), 0) >= lax.broadcasted_iota(jnp.int32, (chunk, chunk), 1)
    gn = gn_ref[...]

    def factored_chunk(c, carry):
        rows = pl.ds(pl.multiple_of(c * chunk, chunk), chunk)
        g = logf_ref[0, rows, :]
        cum = _chunk_cumsum(g, chunk)
        total = cum[chunk - 1:chunk, :]
        fk = 1.0 - jnp.exp(g)
        q_in = (fq_ref[0, rows, :].astype(F32) * jnp.exp(cum)).astype(BF16)
        k_out = (fk * jnp.exp(-cum)).astype(BF16)
        k_end = (fk * jnp.exp(total - cum)).astype(BF16)
        dec = jnp.exp(total)
        v = vi_ref[0, rows, :]
        for h in range(heads):
            sl = slice(h * HG_W, (h + 1) * HG_W)
            st = st_ref[h]
            o = lax.dot_general(q_in[:, sl], st.astype(BF16), (((1,), (1,)), ((), ())), preferred_element_type=F32)
            a = lax.dot_general(q_in[:, sl], k_out[:, sl], (((1,), (1,)), ((), ())), preferred_element_type=F32)
            a = jnp.where(tri, a, 0.0).astype(BF16)
            o = o + jnp.dot(a, v[:, sl], preferred_element_type=F32)
            st_ref[h] = st * dec[:, sl] + lax.dot_general(v[:, sl], k_end[:, sl], (((0,), (0,)), ((), ())),
                                                          preferred_element_type=F32)
            o = _rms(o, gn) * og_ref[0, rows, sl].astype(F32)
            ob_ref[0, rows, sl] = o.astype(BF16)
        return carry

    def exact_chunk(c, carry):
        _gla_exact_chunk(c * chunk, chunk, heads, fq_ref, logf_ref, vi_ref, og_ref, ob_ref, st_ref, gn)
        return carry

    chunk_totals = jnp.sum(logf_ref[0].reshape(tt // chunk, chunk, -1), axis=1)
    factorable = jnp.min(chunk_totals) >= -GLA_MAX_CHUNK_DECAY

    @pl.when(factorable)
    def _():
        lax.fori_loop(0, tt // chunk, factored_chunk, 0)

    @pl.when(jnp.logical_not(factorable))
    def _():
        lax.fori_loop(0, tt // chunk, exact_chunk, 0)

    @pl.when(t == pl.num_programs(1) - 1)
    def _():
        for h in range(heads):
            s_out_ref[0, h] = st_ref[h].T


def _gla(fq, logf, vi, og, s0, gn, *, tt, chunk):
    b, t, w = fq.shape
    heads = w // HG_W
    tok = pl.BlockSpec((1, tt, w), lambda i, j: (i, j, 0))
    st = pl.BlockSpec((1, heads, HG_W, HG_W), lambda i, j: (i, 0, 0, 0))
    return pl.pallas_call(
        functools.partial(_gla_kernel, chunk=chunk, heads=heads),
        grid=(b, t // tt),
        in_specs=[tok, tok, tok, tok, st, _resident((1, HG_W))],
        out_specs=[tok, st],
        out_shape=[jax.ShapeDtypeStruct((b, t, w), BF16), jax.ShapeDtypeStruct(s0.shape, F32)],
        scratch_shapes=[pltpu.VMEM((heads, HG_W, HG_W), F32)],
        compiler_params=_cparams("parallel", "arbitrary"), name="gla",
    )(fq, logf, vi, og, s0, gn)


def _diff_norm(o0, o1, lam, gn, out_scale, axis):
    o = o0 - lam * o1
    return o * lax.rsqrt(jnp.mean(o * o, axis=axis, keepdims=True) + EPS) * gn * out_scale


def _prompt_attn_kernel(q_ref, k_ref, vt_ref, lam_ref, gn_ref, o_ref, m_ref, l_ref, acc_ref, s0_ref, s1_ref, *, out_scale):
    qi = pl.program_id(2)
    tq = q_ref.shape[1]
    tk = s0_ref.shape[0]
    nk = tq // tk
    q = q_ref[0]
    lane = lax.broadcasted_iota(jnp.int32, q.shape, 1)
    zero = jnp.zeros_like(q)
    qm = (jnp.where(lane < HEAD_DIM, q, zero), jnp.where(lane >= HEAD_DIM, q, zero))
    s_refs = (s0_ref, s1_ref)
    m_ref[...] = jnp.full(m_ref.shape, NEG, F32)
    l_ref[...] = jnp.zeros(l_ref.shape, F32)
    acc_ref[...] = jnp.zeros(acc_ref.shape, F32)

    def rows(j):
        return pl.ds(pl.multiple_of(j * tk, tk), tk)

    def scores(j, mp, lo=0):
        s_refs[mp][:, :tq - lo] = lax.dot_general(k_ref[0, rows(j), :], qm[mp][lo:], (((1,), (1,)), ((), ())),
                                                  preferred_element_type=F32)

    def consume(j, mp, mask=None, lo=0):
        cols = slice(lo, tq)
        s = s_refs[mp][:, :tq - lo]
        if mask is not None:
            s = jnp.where(mask, s, NEG)
        m_old = m_ref[mp, :, cols]
        m_new = jnp.maximum(m_old, jnp.max(s, axis=0, keepdims=True))
        p = jnp.exp2(s - m_new)
        if mask is not None:
            p = jnp.where(mask, p, 0.0)
        alpha = jnp.exp2(m_old - m_new)
        l_ref[mp, :, cols] = l_ref[mp, :, cols] * alpha + jnp.sum(p, axis=0, keepdims=True)
        acc_ref[mp, :, cols] = acc_ref[mp, :, cols] * alpha + jnp.dot(vt_ref[0, 0, :, rows(j)], p.astype(BF16),
                                                                     preferred_element_type=F32)
        m_ref[mp, :, cols] = m_new

    def full_tile(j):
        scores(j, 1)
        consume(j, 0)
        scores(j + 1, 0)
        consume(j, 1)

    def single(j, carry):
        full_tile(j)
        return carry

    def group(i, carry):
        for u in range(ATTN_UNROLL):
            full_tile(rem + ATTN_UNROLL * i + u)
        return carry

    nfull = nk * qi
    rem = nfull % ATTN_UNROLL
    scores(0, 0)
    lax.fori_loop(0, rem, single, 0)
    lax.fori_loop(0, nfull // ATTN_UNROLL, group, 0)
    for d in range(nk):
        lo = d * tk
        causal = lax.broadcasted_iota(jnp.int32, (tk, tq - lo), 0) <= lax.broadcasted_iota(jnp.int32, (tk, tq - lo), 1)
        scores(nfull + d, 1, lo)
        consume(nfull + d, 0, causal, lo)
        if d + 1 < nk:
            scores(nfull + d + 1, 0, lo + tk)
        consume(nfull + d, 1, causal, lo)

    o = _diff_norm(acc_ref[0] / l_ref[0], acc_ref[1] / l_ref[1], lam_ref[...], gn_ref[...], out_scale, 0)
    o_ref[0] = o.T.astype(BF16)


def _prompt_attn(q, kb, vt, lam_col, gn_col, *, tq, tk, out_scale):
    b, s, w = q.shape
    heads = w // HEAD_W
    assert tq % tk == 0 and s % tq == 0
    return pl.pallas_call(
        functools.partial(_prompt_attn_kernel, out_scale=out_scale),
        grid=(b, heads, s // tq),
        in_specs=[pl.BlockSpec((1, tq, HEAD_W), lambda i, h, j: (i, j, h)),
                  pl.BlockSpec((1, s, HEAD_W), lambda i, h, j: (i, 0, h)),
                  pl.BlockSpec((1, 1, HEAD_W, s), lambda i, h, j: (i, h, 0, 0)),
                  _resident((HEAD_W, 1)), _resident((HEAD_W, 1))],
        out_specs=pl.BlockSpec((1, tq, HEAD_W), lambda i, h, j: (i, j, h)),
        out_shape=jax.ShapeDtypeStruct((b, s, w), BF16),
        scratch_shapes=[pltpu.VMEM((2, 1, tq), F32), pltpu.VMEM((2, 1, tq), F32), pltpu.VMEM((2, HEAD_W, tq), F32),
                        pltpu.VMEM((tk, tq), F32), pltpu.VMEM((tk, tq), F32)],
        compiler_params=_cparams("parallel", "parallel", "arbitrary"), name="prompt_attn",
    )(q, kb, vt, lam_col, gn_col)


def _sample_attn_kernel(pt_ref, qbd_ref, *refs, pages, t_new, heads, out_scale):
    k_refs, v_refs = refs[:pages], refs[pages:2 * pages]
    kn_ref, vn_ref, lam_ref, gn_ref, o_ref, m_ref, l_ref, acc_ref = refs[2 * pages:]
    pg = pl.program_id(1)

    @pl.when(pg == 0)
    def _():
        m_ref[...] = jnp.full(m_ref.shape, NEG, F32)
        l_ref[...] = jnp.zeros(l_ref.shape, F32)
        acc_ref[...] = jnp.zeros(acc_ref.shape, F32)

    qbd = qbd_ref[0]

    def update(s, v, mask):
        if mask is not None:
            s = jnp.where(mask, s, NEG)
        m_old = m_ref[...]
        m_new = jnp.maximum(m_old, jnp.max(s, axis=1, keepdims=True))
        p = jnp.exp2(s - m_new)
        if mask is not None:
            p = jnp.where(mask, p, 0.0)
        alpha = jnp.exp2(m_old - m_new)
        l_ref[...] = l_ref[...] * alpha + jnp.sum(p, axis=1, keepdims=True)
        acc_ref[...] = acc_ref[...] * alpha + jnp.dot(p.astype(BF16), v, preferred_element_type=F32)
        m_ref[...] = m_new

    w = qbd.shape[1]
    kt = jnp.concatenate([r[0, 0].reshape(w, PAGE).astype(BF16) for r in k_refs], axis=1)
    v = jnp.concatenate(
        [jnp.concatenate([r[0, 0, pl.ds(h, PAGE, stride=heads), :] for h in range(heads)], axis=1).astype(BF16)
         for r in v_refs], axis=0)
    update(jnp.dot(qbd, kt, preferred_element_type=F32), v, None)

    @pl.when(pg == pl.num_programs(1) - 1)
    def _():
        rows = qbd.shape[0]
        tpad = kn_ref.shape[1]
        tq = lax.broadcasted_iota(jnp.int32, (rows, tpad), 0) % t_new
        tk = lax.broadcasted_iota(jnp.int32, (rows, tpad), 1)
        s_new = lax.dot_general(qbd, kn_ref[0].astype(BF16), (((1,), (1,)), ((), ())), preferred_element_type=F32)
        update(s_new, vn_ref[0].astype(BF16), tk <= tq)
        out = acc_ref[...] / l_ref[...]
        lam, gn = lam_ref[...], gn_ref[...]
        for h in range(heads):
            blk = out[h * 2 * t_new:(h + 1) * 2 * t_new, h * HEAD_W:(h + 1) * HEAD_W]
            o_ref[0, :, h * HEAD_W:(h + 1) * HEAD_W] = _diff_norm(blk[:t_new], blk[t_new:], lam, gn, out_scale, 1)


def _sample_attn(page_table, qbd, cache_k, cache_v, k_new, v_new, lam_row, gn_row, *, layer, pages, t_new, out_scale):
    b, rows, w = qbd.shape
    heads = w // HEAD_W
    n_pages = page_table.shape[1]
    tpad = k_new.shape[1]

    def k_spec(i):
        return pl.BlockSpec((1, 1, heads, 2, HEAD_DIM, PAGE), lambda bi, pg, pt: (layer, pt[bi, pg * pages + i], 0, 0, 0, 0))

    def v_spec(i):
        return pl.BlockSpec((1, 1, PAGE * heads, HEAD_W), lambda bi, pg, pt: (layer, pt[bi, pg * pages + i], 0, 0))

    per_b = lambda shape: pl.BlockSpec(shape, lambda bi, pg, pt: (bi, 0, 0))
    const = lambda shape: pl.BlockSpec(shape, lambda bi, pg, pt: (0, 0))
    grid_spec = pltpu.PrefetchScalarGridSpec(
        num_scalar_prefetch=1, grid=(b, n_pages // pages),
        in_specs=[per_b((1, rows, w))] + [k_spec(i) for i in range(pages)] + [v_spec(i) for i in range(pages)]
                 + [per_b((1, tpad, w)), per_b((1, tpad, w)), const((1, HEAD_W)), const((1, HEAD_W))],
        out_specs=per_b((1, t_new, w)),
        scratch_shapes=[pltpu.VMEM((rows, 1), F32), pltpu.VMEM((rows, 1), F32), pltpu.VMEM((rows, w), F32)])
    return pl.pallas_call(
        functools.partial(_sample_attn_kernel, pages=pages, t_new=t_new, heads=heads, out_scale=out_scale),
        grid_spec=grid_spec,
        out_shape=jax.ShapeDtypeStruct((b, t_new, w), F32),
        compiler_params=_cparams("parallel", "arbitrary"), name="sample_attn",
    )(page_table, qbd, *([cache_k] * pages), *([cache_v] * pages), k_new, v_new, lam_row, gn_row)


def _mix_ffn_kernel(x_ref, oa_ref, ob_ref, pe_ref, g_pre_ref, w_gate_ref, w_o_ref, g_postmix_ref, g_preffn_ref,
                    w_ff_in_ref, w_ff_out_ref, g_postffn_ref, w_pe_ref, w_pg_ref, y_ref, *, d_model, d_ff):
    x = x_ref[...]
    h = _rms(x, g_pre_ref[...]).astype(BF16)
    gates = _sigmoid(jnp.dot(h, w_gate_ref[...], preferred_element_type=F32))
    m = gates[:, :d_model] * oa_ref[...].astype(F32) + gates[:, d_model:] * ob_ref[...].astype(F32)
    x = x + _rms(jnp.dot(m.astype(BF16), w_o_ref[...], preferred_element_type=F32), g_postmix_ref[...])
    h = _rms(x, g_preffn_ref[...]).astype(BF16)
    g = jnp.dot(h, w_ff_in_ref[:, :d_ff], preferred_element_type=F32)
    u = jnp.dot(h, w_ff_in_ref[:, d_ff:], preferred_element_type=F32)
    act = (g * _sigmoid(g) * u).astype(BF16)
    x = x + _rms(jnp.dot(act, w_ff_out_ref[...], preferred_element_type=F32), g_postffn_ref[...])
    pg = _sigmoid(jnp.dot(x.astype(BF16), w_pg_ref[...], preferred_element_type=F32))
    y_ref[...] = x + pg * jnp.dot(pe_ref[...].astype(BF16), w_pe_ref[...], preferred_element_type=F32)


def _mix_ffn(x, oa, ob, pemb, g_pre, w_gate, w_o, g_postmix, g_preffn, w_ff_in, w_ff_out, g_postffn, w_pe, w_pg, *, tm):
    n, d = x.shape
    d_ff = w_ff_out.shape[0]
    tok = lambda w: pl.BlockSpec((tm, w), lambda t: (t, 0))
    weights = (g_pre, w_gate, w_o, g_postmix, g_preffn, w_ff_in, w_ff_out, g_postffn, w_pe, w_pg)
    return pl.pallas_call(
        functools.partial(_mix_ffn_kernel, d_model=d, d_ff=d_ff),
        grid=(n // tm,),
        in_specs=[tok(d), tok(d), tok(d), tok(pemb.shape[1])] + [_resident(a.shape) for a in weights],
        out_specs=tok(d), out_shape=jax.ShapeDtypeStruct((n, d), F32),
        compiler_params=_cparams("parallel"), name="mix_ffn",
    )(x, oa, ob, pemb, *weights)


def _rope_tables(pos):
    half = HEAD_DIM // 2
    inv = 1.0 / (ROPE_THETA ** (jnp.arange(half, dtype=F32) / half))
    ang = pos.astype(F32)[:, None] * inv[None, :]
    cos, sin = jnp.cos(ang), jnp.sin(ang)
    return jnp.tile(jnp.concatenate([cos, cos], -1), (1, 2)), jnp.tile(jnp.concatenate([-sin, sin], -1), (1, 2))


def _tile(n, cap):
    t = min(n, cap)
    assert n % t == 0, (n, cap)
    return t


def kernel(x_prompt, x_sample, cache_k, cache_v, state_hgrn, page_table, p_prompt, p_sample, pre_mix_g, w_in, lam_q1, lam_k1, lam_q2, lam_k2, attn_norm_g, hgrn_lb, hgrn_norm_g, w_o, post_mix_g, pre_ffn_g, w_ff_in, w_ff_out, post_ffn_g, w_pe, w_pg):
    depth = w_in.shape[0]
    bp, sp, d = x_prompt.shape
    bs, ts, _ = x_sample.shape
    heads = d // HEAD_W
    n_pages = page_table.shape[1]
    assert d % HEAD_W == 0 and cache_k.shape[2] == PAGE
    lb_all = jnp.cumsum(jax.nn.softmax(hgrn_lb.astype(F32), axis=0), axis=0)
    cos_p, sin_p = _rope_tables(jnp.arange(sp))
    cos_s, sin_s = _rope_tables(jnp.tile(n_pages * PAGE + jnp.arange(ts), bs))
    ck = jnp.transpose(cache_k, (0, 1, 3, 4, 5, 2))
    cv = cache_v.reshape(cache_v.shape[0], cache_v.shape[1], PAGE * heads, HEAD_W)
    ts_pad = -(-ts // 8) * 8
    pad_t = lambda a: jnp.pad(a.reshape(bs, ts, d), ((0, 0), (0, ts_pad - ts), (0, 0)))
    eye = jnp.eye(2 * heads, dtype=BF16)

    yp, ys = x_prompt.reshape(bp * sp, d), x_sample.reshape(bs * ts, d)
    outs = [[] for _ in range(6)]
    for i in range(depth):
        row = lambda a: a[i].reshape(1, -1).astype(F32)
        wi = w_in[i].astype(BF16)
        w_qkv, w_h, w_gate = wi[:, :3 * d], wi[:, 3 * d:7 * d], wi[:, 7 * d:]
        lam_init = 0.8 - 0.6 * math.exp(-0.3 * i)
        lam = (jnp.exp(jnp.sum(lam_q1[i].astype(F32) * lam_k1[i].astype(F32)))
               - jnp.exp(jnp.sum(lam_q2[i].astype(F32) * lam_k2[i].astype(F32))) + lam_init)
        lam_row = jnp.full((1, HEAD_W), lam, F32)
        gn_row = row(attn_norm_g)
        lb = lb_all[i].reshape(1, d)
        mix_w = (row(pre_mix_g), w_gate, w_o[i].astype(BF16), row(post_mix_g), row(pre_ffn_g), w_ff_in[i].astype(BF16),
                 w_ff_out[i].astype(BF16), row(post_ffn_g), w_pe[i].astype(BF16), w_pg[i].astype(BF16))

        q, k, v, kb, vt = _qkv_proj(yp, row(pre_mix_g), w_qkv, cos_p, sin_p, seqs=bp, tm=_tile(sp, 512), attn_copies=True)
        oa = _prompt_attn(q.reshape(bp, sp, d), kb.reshape(bp, sp, d), vt, lam_row.reshape(HEAD_W, 1),
                          gn_row.reshape(HEAD_W, 1), tq=_tile(sp, 1024), tk=_tile(sp, 512), out_scale=1.0 - lam_init)
        fq, logf, vi, og = _hgrn_proj(yp, row(pre_mix_g), w_h, lb, tm=_tile(bp * sp, 1024))
        r3 = lambda a: a.reshape(bp, sp, d)
        ob, s_p = _gla(r3(fq), r3(logf), r3(vi), r3(og), jnp.zeros((bp, d // HG_W, HG_W, HG_W), F32),
                       row(hgrn_norm_g), tt=_tile(sp, 512), chunk=GLA_CHUNK)
        yp = _mix_ffn(yp, oa.reshape(bp * sp, d), ob.reshape(bp * sp, d), p_prompt[i].reshape(bp * sp, -1), *mix_w,
                      tm=_tile(bp * sp, 512))
        k = jnp.transpose(k.reshape(bp, heads, 2, HEAD_DIM, sp), (0, 4, 1, 2, 3))
        outs[0].append(k); outs[1].append(v.reshape(bp, sp, heads, HEAD_W)); outs[2].append(s_p)

        q, k, v = _qkv_proj(ys, row(pre_mix_g), w_qkv, cos_s, sin_s, seqs=1, tm=_tile(bs * ts, 512), attn_copies=False)
        qt = q.reshape(bs, ts, 2 * heads, HEAD_DIM).transpose(0, 2, 1, 3)
        qbd = (qt[:, :, :, None, :] * eye[None, :, None, :, None]).reshape(bs, 2 * heads * ts, d)
        oa = _sample_attn(page_table, qbd, ck, cv, pad_t(k), pad_t(v), lam_row, gn_row, layer=i,
                          pages=math.gcd(n_pages, 16), t_new=ts, out_scale=1.0 - lam_init)
        fq, logf, vi, og = _hgrn_proj(ys, row(pre_mix_g), w_h, lb, tm=_tile(bs * ts, 512))
        ob, s_s = _gla(pad_t(fq), pad_t(logf), pad_t(vi), pad_t(og), state_hgrn[i].astype(F32), row(hgrn_norm_g),
                       tt=ts_pad, chunk=ts_pad)
        ys = _mix_ffn(ys, oa.reshape(bs * ts, d).astype(BF16), ob[:, :ts].reshape(bs * ts, d), p_sample[i].reshape(bs * ts, -1),
                      *mix_w, tm=_tile(bs * ts, 256))
        outs[3].append(k.reshape(bs, ts, heads, 2, HEAD_DIM)); outs[4].append(v.reshape(bs, ts, heads, HEAD_W)); outs[5].append(s_s)

    st = [jnp.stack(o) for o in outs]
    return (yp.reshape(bp, sp, d), ys.reshape(bs, ts, d), st[0], st[1], st[2], st[3], st[4], st[5])
```
